```python
import math
import jax, jax.numpy as jnp
from jax import lax
import numpy as np

D_MODEL = 2048
BATCH = 4
SEQ = 2048
DEPTH = 2
DEC_BATCH = 8
DEC_SEQ = 4
PAST_LEN = 16384
PAGE_SIZE = 128

N_META = 16
N_EVEN = (DEPTH + 1) // 2
N_ODD = DEPTH // 2
D_ATT = D_MODEL // 2
N_DH = 8
DK = D_ATT // N_DH // 2
DV = 2 * DK
D_CONV = D_MODEL - D_ATT
CONV_W = 31
Q_BLOCK = 128
D_SSM = D_MODEL
SSM_GROUP = 16
N_SSM_GROUPS = D_SSM // SSM_GROUP
SSM_P = 64
D_FF = ((8 * D_MODEL // 3 + 255) // 256) * 256
EPS = 1e-6

kernel_name = 'hybrid_diffattn_conformer_s5_step'


def rms_norm(x, g):
    xf = x.astype(jnp.float32)
    y = xf * lax.rsqrt(jnp.mean(xf * xf, axis=-1, keepdims=True) + EPS)
    return (y * g.astype(jnp.float32)).astype(x.dtype)


def layer_norm(x, g, b):
    xf = x.astype(jnp.float32)
    mu = jnp.mean(xf, axis=-1, keepdims=True)
    var = jnp.mean(jnp.square(xf - mu), axis=-1, keepdims=True)
    y = (xf - mu) * lax.rsqrt(var + EPS)
    return (y * g.astype(jnp.float32) + b.astype(jnp.float32)).astype(x.dtype)


def alibi_slopes(n):
    return jnp.array([2.0 ** (-8.0 * (i + 1) / n) for i in range(n)], dtype=jnp.float32)


def diff_attn_core(q, qpos, segs, lam):
    slopes = alibi_slopes(N_DH)
    qf = q.astype(jnp.float32) * (DK ** -0.5)
    scores = []
    for k, v, kpos in segs:
        s = jnp.einsum('bqhmd,bshmd->bhmqs', qf, k.astype(jnp.float32))
        dist = (qpos[:, None] - kpos[None, :]).astype(jnp.float32)
        s = s - slopes[None, :, None, None, None] * jnp.abs(dist)
        scores.append(jnp.where(dist >= 0, s, -jnp.inf))
    p = jax.nn.softmax(jnp.concatenate(scores, axis=-1), axis=-1)
    out = 0.0
    off = 0
    for k, v, kpos in segs:
        n = kpos.shape[0]
        out = out + jnp.einsum('bhmqs,bshe->bqhme', p[..., off:off + n], v.astype(jnp.float32))
        off += n
    return out[:, :, :, 0] - lam * out[:, :, :, 1]


def prompt_diff_attention(q, k, v, lam):
    B, T = q.shape[0], q.shape[1]
    n_blk = -(-T // Q_BLOCK)
    t_pad = n_blk * Q_BLOCK
    qb = jnp.pad(q, ((0, 0), (0, t_pad - T), (0, 0), (0, 0), (0, 0)))
    qb = qb.reshape(B, n_blk, Q_BLOCK, N_DH, 2, DK).swapaxes(0, 1)
    kpos = jnp.arange(T, dtype=jnp.int32)

    def one(args):
        q_blk, start = args
        qpos = start + jnp.arange(Q_BLOCK, dtype=jnp.int32)
        return diff_attn_core(q_blk, qpos, [(k, v, kpos)], lam)

    o = lax.map(one, (qb, jnp.arange(n_blk, dtype=jnp.int32) * Q_BLOCK))
    return o.swapaxes(0, 1).reshape(B, t_pad, N_DH, DV)[:, :T]


def causal_dwconv(g_ext, w, b):
    out = lax.conv_general_dilated(g_ext, w.astype(g_ext.dtype)[:, None, :], window_strides=(1,),
                                   padding='VALID', dimension_numbers=('NWC', 'WIO', 'NWC'),
                                   feature_group_count=g_ext.shape[-1])
    return out + b.astype(g_ext.dtype)


def even_mixer(h, e, prm, past, conv_prev):
    B, T, _ = h.shape
    proj = h @ prm['w_in_even'][e]
    q, k, v, cv, cg = jnp.split(proj, [D_ATT, 2 * D_ATT, 3 * D_ATT, 3 * D_ATT + D_CONV], axis=-1)
    q = q.reshape(B, T, N_DH, 2, DK)
    k = k.reshape(B, T, N_DH, 2, DK)
    v = v.reshape(B, T, N_DH, DV)
    lam_init = 0.8 - 0.6 * math.exp(-0.3 * (2 * e))
    lq = prm['lambda_q'][e].astype(jnp.float32)
    lk = prm['lambda_k'][e].astype(jnp.float32)
    lam = jnp.exp(jnp.sum(lq[0] * lk[0])) - jnp.exp(jnp.sum(lq[1] * lk[1])) + lam_init
    if past is None:
        o = prompt_diff_attention(q, k, v, lam)
    else:
        cache_k, cache_v, page_table = past
        past_len = page_table.shape[1] * PAGE_SIZE
        k_past = cache_k[e, page_table].reshape(B, past_len, N_DH, 2, DK)
        v_past = cache_v[e, page_table].reshape(B, past_len, N_DH, DV)
        qpos = past_len + jnp.arange(T, dtype=jnp.int32)
        o = diff_attn_core(q, qpos, [(k_past, v_past, jnp.arange(past_len, dtype=jnp.int32)), (k, v, qpos)], lam)
    o = (rms_norm(o, prm['subln_g'][e]) * (1.0 - lam_init)).reshape(B, T, D_ATT).astype(h.dtype)
    g = cv * jax.nn.sigmoid(cg)
    if conv_prev is None:
        conv_prev = jnp.zeros((B, CONV_W - 1, D_CONV), g.dtype)
    g_ext = jnp.concatenate([conv_prev.astype(g.dtype), g], axis=1)
    c = causal_dwconv(g_ext, prm['conv_w'][e], prm['conv_b'][e])
    c = jax.nn.silu(layer_norm(c, prm['conv_ln_g'][e], prm['conv_ln_b'][e]))
    y = jnp.concatenate([o, c], axis=-1) @ prm['w_out_even'][e]
    return y, k, v, g_ext[:, -(CONV_W - 1):]


def odd_mixer(h, o, prm, h0_re, h0_im):
    B, T, _ = h.shape
    f32 = jnp.float32
    u = (h @ prm['w_in_odd'][o]).reshape(B, T, N_SSM_GROUPS, SSM_GROUP).astype(f32)
    lam_c = lax.complex(prm['ssm_a_re'][o].astype(f32), prm['ssm_a_im'][o].astype(f32))
    dt = jnp.exp(prm['ssm_log_dt'][o].astype(f32))[:, None]
    abar = jnp.exp(lam_c * dt)
    coef = (abar - 1.0) / lam_c
    b_re = prm['ssm_b_re'][o].astype(f32)
    b_im = prm['ssm_b_im'][o].astype(f32)
    bb_re = coef.real[..., None] * b_re - coef.imag[..., None] * b_im
    bb_im = coef.real[..., None] * b_im + coef.imag[..., None] * b_re
    bu = lax.complex(jnp.einsum('gpc,btgc->btgp', bb_re, u), jnp.einsum('gpc,btgc->btgp', bb_im, u))
    h0 = lax.complex(h0_re.astype(f32), h0_im.astype(f32))
    bu = bu.at[:, 0].add(abar * h0)
    a_seq = jnp.broadcast_to(abar, (1, T, N_SSM_GROUPS, SSM_P))

    def comb(l, r):
        a1, b1 = l
        a2, b2 = r
        return (a1 * a2, a2 * b1 + b2)

    _, states = lax.associative_scan(comb, (a_seq, bu), axis=1)
    y = (jnp.einsum('gcp,btgp->btgc', prm['ssm_c_re'][o].astype(f32), states.real)
         - jnp.einsum('gcp,btgp->btgc', prm['ssm_c_im'][o].astype(f32), states.imag)
         + prm['ssm_d'][o].astype(f32) * u)
    y = jax.nn.gelu(y.reshape(B, T, D_SSM)).astype(h.dtype)
    y = y * jax.nn.sigmoid(y @ prm['w_glu'][o])
    last = states[:, -1]
    return y @ prm['w_out_odd'][o], last.real, last.imag


def swiglu(h, wg, wu, wd):
    return (jax.nn.silu(h @ wg) * (h @ wu)) @ wd


def run_trunk(x, prm, past, conv_prev, ssm_prev_re, ssm_prev_im):
    B = x.shape[0]
    ks, vs, convs, sres, sims = [], [], [], [], []
    for l in range(DEPTH):
        h = rms_norm(x, prm['norm_mix_pre'][l])
        if l % 2 == 0:
            e = l // 2
            cp = None if conv_prev is None else conv_prev[e]
            m, k, v, c = even_mixer(h, e, prm, past, cp)
            ks.append(k)
            vs.append(v)
            convs.append(c)
        else:
            o = l // 2
            if ssm_prev_re is None:
                h0r = jnp.zeros((B, N_SSM_GROUPS, SSM_P), jnp.float32)
                h0i = h0r
            else:
                h0r, h0i = ssm_prev_re[o], ssm_prev_im[o]
            m, sr, si = odd_mixer(h, o, prm, h0r, h0i)
            sres.append(sr.astype(x.dtype))
            sims.append(si.astype(x.dtype))
        x = x + rms_norm(m, prm['norm_mix_post'][l])
        h = rms_norm(x, prm['norm_ffn_pre'][l])
        f = swiglu(h, prm['w_ffn_gate'][l], prm['w_ffn_up'][l], prm['w_ffn_down'][l])
        x = x + rms_norm(f, prm['norm_ffn_post'][l])
    return x, jnp.stack(ks), jnp.stack(vs), jnp.stack(convs), jnp.stack(sres), jnp.stack(sims)


def setup_inputs(seed: int = 0) -> dict:
    key = jax.random.key(seed)
    keys = iter(jax.random.split(key, 48))
    f32 = jnp.float32

    def nrm(shape, scale):
        return jax.random.normal(next(keys), shape, f32) * scale

    n_pages = PAST_LEN // PAGE_SIZE
    n_used = DEC_BATCH * n_pages
    n_pool = (5 * n_used + 3) // 4
    G, P = N_SSM_GROUPS, SSM_P
    inp = {}
    inp['x_prompt'] = nrm((BATCH, SEQ, D_MODEL), 1.0)
    inp['x_sample'] = nrm((DEC_BATCH, DEC_SEQ, D_MODEL), 1.0)
    inp['cache_k'] = nrm((N_EVEN, n_pool, PAGE_SIZE, N_DH, 2, DK), 1.0)
    inp['cache_v'] = nrm((N_EVEN, n_pool, PAGE_SIZE, N_DH, DV), 1.0)
    inp['state_conv'] = nrm((N_EVEN, DEC_BATCH, CONV_W - 1, D_CONV), 0.5)
    inp['state_ssm_re'] = nrm((N_ODD, DEC_BATCH, G, P), 0.1)
    inp['state_ssm_im'] = nrm((N_ODD, DEC_BATCH, G, P), 0.1)
    perm = jax.random.permutation(next(keys), n_pool)[:n_used]
    inp['page_table'] = perm.reshape(DEC_BATCH, n_pages).astype(jnp.int32)
    inp['meta_tokens'] = nrm((N_META, D_MODEL), 1.0)
    inp['norm_mix_pre'] = 1.0 + nrm((DEPTH, D_MODEL), 0.02)
    inp['norm_mix_post'] = 1.0 + nrm((DEPTH, D_MODEL), 0.02)
    inp['norm_ffn_pre'] = 1.0 + nrm((DEPTH, D_MODEL), 0.02)
    inp['norm_ffn_post'] = 1.0 + nrm((DEPTH, D_MODEL), 0.02)
    inp['w_in_even'] = nrm((N_EVEN, D_MODEL, 3 * D_ATT + 2 * D_CONV), D_MODEL ** -0.5)
    inp['lambda_q'] = nrm((N_EVEN, 2, DK), 0.1)
    inp['lambda_k'] = nrm((N_EVEN, 2, DK), 0.1)
    inp['subln_g'] = 1.0 + nrm((N_EVEN, DV), 0.02)
    inp['conv_w'] = nrm((N_EVEN, CONV_W, D_CONV), CONV_W ** -0.5)
    inp['conv_b'] = nrm((N_EVEN, D_CONV), 0.01)
    inp['conv_ln_g'] = 1.0 + nrm((N_EVEN, D_CONV), 0.02)
    inp['conv_ln_b'] = nrm((N_EVEN, D_CONV), 0.01)
    inp['w_out_even'] = nrm((N_EVEN, D_ATT + D_CONV, D_MODEL), (D_ATT + D_CONV) ** -0.5)
    inp['w_in_odd'] = nrm((N_ODD, D_MODEL, D_SSM), D_MODEL ** -0.5)
    inp['ssm_a_re'] = -0.5 * jnp.exp(nrm((N_ODD, G, P), 0.1))
    inp['ssm_a_im'] = math.pi * jnp.arange(P, dtype=f32)[None, None, :] + nrm((N_ODD, G, P), 0.01)
    inp['ssm_b_re'] = nrm((N_ODD, G, P, SSM_GROUP), (2.0 * SSM_GROUP) ** -0.5)
    inp['ssm_b_im'] = nrm((N_ODD, G, P, SSM_GROUP), (2.0 * SSM_GROUP) ** -0.5)
    inp['ssm_c_re'] = nrm((N_ODD, G, SSM_GROUP, P), (2.0 * P) ** -0.5)
    inp['ssm_c_im'] = nrm((N_ODD, G, SSM_GROUP, P), (2.0 * P) ** -0.5)
    inp['ssm_d'] = nrm((N_ODD, G, SSM_GROUP), 1.0)
    inp['ssm_log_dt'] = jax.random.uniform(next(keys), (N_ODD, G), f32, math.log(1e-3), math.log(1e-1))
    inp['w_glu'] = nrm((N_ODD, D_SSM, D_SSM), D_SSM ** -0.5)
    inp['w_out_odd'] = nrm((N_ODD, D_SSM, D_MODEL), D_SSM ** -0.5)
    inp['w_ffn_gate'] = nrm((DEPTH, D_MODEL, D_FF), D_MODEL ** -0.5)
    inp['w_ffn_up'] = nrm((DEPTH, D_MODEL, D_FF), D_MODEL ** -0.5)
    inp['w_ffn_down'] = nrm((DEPTH, D_FF, D_MODEL), D_FF ** -0.5)
    return inp


def reference(x_prompt, x_sample, cache_k, cache_v, state_conv, state_ssm_re, state_ssm_im, page_table,
              meta_tokens, norm_mix_pre, norm_mix_post, norm_ffn_pre, norm_ffn_post,
              w_in_even, lambda_q, lambda_k, subln_g, conv_w, conv_b, conv_ln_g, conv_ln_b, w_out_even,
              w_in_odd, ssm_a_re, ssm_a_im, ssm_b_re, ssm_b_im, ssm_c_re, ssm_c_im, ssm_d, ssm_log_dt,
              w_glu, w_out_odd, w_ffn_gate, w_ffn_up, w_ffn_down):
    prm = dict(norm_mix_pre=norm_mix_pre, norm_mix_post=norm_mix_post, norm_ffn_pre=norm_ffn_pre,
               norm_ffn_post=norm_ffn_post, w_in_even=w_in_even, lambda_q=lambda_q, lambda_k=lambda_k,
               subln_g=subln_g, conv_w=conv_w, conv_b=conv_b, conv_ln_g=conv_ln_g, conv_ln_b=conv_ln_b,
               w_out_even=w_out_even, w_in_odd=w_in_odd, ssm_a_re=ssm_a_re, ssm_a_im=ssm_a_im,
               ssm_b_re=ssm_b_re, ssm_b_im=ssm_b_im, ssm_c_re=ssm_c_re, ssm_c_im=ssm_c_im, ssm_d=ssm_d,
               ssm_log_dt=ssm_log_dt, w_glu=w_glu, w_out_odd=w_out_odd, w_ffn_gate=w_ffn_gate,
               w_ffn_up=w_ffn_up, w_ffn_down=w_ffn_down)
    B = x_prompt.shape[0]
    meta = jnp.broadcast_to(meta_tokens.astype(x_prompt.dtype)[None], (B, N_META, x_prompt.shape[-1]))
    xp = jnp.concatenate([meta, x_prompt], axis=1)
    yp, k_prompt, v_prompt, conv_prompt, ssm_re_prompt, ssm_im_prompt = run_trunk(xp, prm, None, None, None, None)
    y_prompt = yp[:, N_META:]
    y_sample, k_sample, v_sample, conv_sample, ssm_re_sample, ssm_im_sample = run_trunk(
        x_sample, prm, (cache_k, cache_v, page_table), state_conv, state_ssm_re, state_ssm_im)
    return (y_prompt, y_sample, k_prompt, v_prompt, k_sample, v_sample, conv_prompt, conv_sample,
            ssm_re_prompt, ssm_im_prompt, ssm_re_sample, ssm_im_sample)
```

```python
import functools
import math

import jax
import jax.numpy as jnp
from jax import lax
from jax.experimental import pallas as pl
from jax.experimental.pallas import tpu as pltpu

F32 = jnp.float32
BF16 = jnp.bfloat16

D_MODEL = 2048
BATCH = 4
SEQ = 2048
DEPTH = 2
DEC_BATCH = 8
DEC_SEQ = 4
PAGE_SIZE = 128
N_META = 16
D_ATT = D_MODEL // 2
N_DH = 8
DK = D_ATT // N_DH // 2
DV = 2 * DK
D_CONV = D_MODEL - D_ATT
CONV_W = 31
SSM_GROUP = 16
N_SSM_GROUPS = D_MODEL // SSM_GROUP
SSM_P = 64
D_FF = ((8 * D_MODEL // 3 + 255) // 256) * 256
EPS = 1e-6

T_P = N_META + SEQ
ROWS_P = BATCH * T_P
ROWS_S = DEC_BATCH * DEC_SEQ
TM = 640
ROWS = 8320
assert ROWS % TM == 0 and ROWS >= ROWS_P + ROWS_S
assert ROWS_P % ROWS_S == 0
SAMPLE_BLK = ROWS_P // ROWS_S

LANES = 128
SUBLANES = 8
VMEM_LIMIT = 56 * 1024 * 1024

SSM_L = 16
SSM_NC = T_P // SSM_L
SSM_M = ROWS // SSM_L
SSM_NJ = D_MODEL // LANES
SSM_G8 = LANES // SSM_GROUP
SSM_SW = SSM_G8 * SSM_P
assert T_P % SSM_L == 0 and ROWS % SSM_L == 0

ATT_T = 256
ATT_NT = SEQ // ATT_T
DEC_PP = 4


def _cparams(sem, vmem=VMEM_LIMIT):
    return pltpu.CompilerParams(dimension_semantics=sem, vmem_limit_bytes=vmem)


def _rms(x, g):
    ms = jnp.mean(x * x, axis=-1, keepdims=True)
    return x * lax.rsqrt(ms + EPS) * g


def _norm_mm_body(x_ref, g_ref, *refs, glu):
    n_w = 2 if glu else 1
    w_refs, o_ref, hb = refs[:n_w], refs[n_w], refs[n_w + 1]

    @pl.when(pl.program_id(1) == 0)
    def _():
        hb[...] = _rms(x_ref[...], g_ref[...]).astype(BF16)

    h = hb[...]
    a = jnp.dot(h, w_refs[0][...], preferred_element_type=F32)
    if glu:
        b = jnp.dot(h, w_refs[1][...], preferred_element_type=F32)
        a = a * jax.nn.sigmoid(b)
    o_ref[...] = a


def _norm_mm(x, g, ws, *, tn=512):
    rows, d = x.shape
    n = ws[0].shape[1]
    glu = len(ws) == 2
    return pl.pallas_call(
        functools.partial(_norm_mm_body, glu=glu),
        grid=(rows // TM, n // tn),
        in_specs=[pl.BlockSpec((TM, d), lambda i, j: (i, 0)),
                  pl.BlockSpec((1, d), lambda i, j: (0, 0))]
                 + [pl.BlockSpec((d, tn), lambda i, j: (0, j)) for _ in ws],
        out_specs=pl.BlockSpec((TM, tn), lambda i, j: (i, j)),
        out_shape=jax.ShapeDtypeStruct((rows, n), F32),
        scratch_shapes=[pltpu.VMEM((TM, d), BF16)],
        compiler_params=_cparams(("parallel", "arbitrary")),
        name="norm_mm_glu" if glu else "norm_mm",
    )(x, g.reshape(1, d), *ws)


def _ffn_body(x_ref, gpre_ref, wg_ref, wu_ref, wd_ref, gpost_ref, o_ref, hb, acc):
    j = pl.program_id(1)

    @pl.when(j == 0)
    def _():
        hb[...] = _rms(x_ref[...], gpre_ref[...]).astype(BF16)

    h = hb[...]
    gate = jnp.dot(h, wg_ref[...], preferred_element_type=F32)
    up = jnp.dot(h, wu_ref[...], preferred_element_type=F32)
    a = (jax.nn.silu(gate) * up).astype(BF16)
    part = jnp.dot(a, wd_ref[...], preferred_element_type=F32)

    @pl.when(j == 0)
    def _():
        acc[...] = part

    @pl.when(j > 0)
    def _():
        acc[...] += part

    @pl.when(j == pl.num_programs(1) - 1)
    def _():
        o_ref[...] = x_ref[...] + _rms(acc[...], gpost_ref[...])


def _ffn(x, gpre, wg, wu, wd, gpost, *, tf=512):
    rows, d = x.shape
    dff = wg.shape[1]
    return pl.pallas_call(
        _ffn_body,
        grid=(rows // TM, dff // tf),
        in_specs=[pl.BlockSpec((TM, d), lambda i, j: (i, 0)),
                  pl.BlockSpec((1, d), lambda i, j: (0, 0)),
                  pl.BlockSpec((d, tf), lambda i, j: (0, j)),
                  pl.BlockSpec((d, tf), lambda i, j: (0, j)),
                  pl.BlockSpec((tf, d), lambda i, j: (j, 0)),
                  pl.BlockSpec((1, d), lambda i, j: (0, 0))],
        out_specs=pl.BlockSpec((TM, d), lambda i, j: (i, 0)),
        out_shape=jax.ShapeDtypeStruct((rows, d), F32),
        scratch_shapes=[pltpu.VMEM((TM, d), BF16), pltpu.VMEM((TM, d), F32)],
        compiler_params=_cparams(("parallel", "arbitrary")),
        name="ffn",
    )(x, gpre.reshape(1, d), wg, wu, wd, gpost.reshape(1, d))


def _odd_out_body(y_ref, res_ref, wglu_ref, wout_ref, gpost_ref, o_ref, yf, hb, acc, *, tf):
    j = pl.program_id(1)
    nj = pl.num_programs(1)

    @pl.when(j == 0)
    def _():
        for jj in range(yf.shape[0]):
            gy = jax.nn.gelu(y_ref[:, jj * tf:(jj + 1) * tf])
            yf[jj] = gy
            hb[:, jj * tf:(jj + 1) * tf] = gy.astype(BF16)

    t = jnp.dot(hb[...], wglu_ref[...], preferred_element_type=F32)
    a = (yf[j] * jax.nn.sigmoid(t)).astype(BF16)
    part = jnp.dot(a, wout_ref[...], preferred_element_type=F32)

    @pl.when(j == 0)
    def _():
        acc[...] = part

    @pl.when(j > 0)
    def _():
        acc[...] += part

    @pl.when(j == nj - 1)
    def _():
        o_ref[...] = res_ref[...] + _rms(acc[...], gpost_ref[...])


def _odd_out(y, res, wglu, wout, gpost, *, tf=512, tm=TM // 2):
    rows, d = y.shape
    return pl.pallas_call(
        functools.partial(_odd_out_body, tf=tf),
        grid=(rows // tm, d // tf),
        in_specs=[pl.BlockSpec((tm, d), lambda i, j: (i, 0)),
                  pl.BlockSpec((tm, d), lambda i, j: (i, 0)),
                  pl.BlockSpec((d, tf), lambda i, j: (0, j)),
                  pl.BlockSpec((tf, d), lambda i, j: (j, 0)),
                  pl.BlockSpec((1, d), lambda i, j: (0, 0))],
        out_specs=pl.BlockSpec((tm, d), lambda i, j: (i, 0)),
        out_shape=jax.ShapeDtypeStruct((rows, d), F32),
        scratch_shapes=[pltpu.VMEM((d // tf, tm, tf), F32), pltpu.VMEM((tm, d), BF16),
                        pltpu.VMEM((tm, d), F32)],
        compiler_params=_cparams(("parallel", "arbitrary")),
        name="odd_out",
    )(y, res, wglu, wout, gpost.reshape(1, d))


def _even_out_body(o_ref, c_ref, lng_ref, lnb_ref, wtop_ref, wbot_ref, gpost_ref, res_ref, out_ref):
    c = c_ref[...]
    mu = jnp.mean(c, axis=-1, keepdims=True)
    var = jnp.mean(jnp.square(c - mu), axis=-1, keepdims=True)
    cn = jax.nn.silu((c - mu) * lax.rsqrt(var + EPS) * lng_ref[...] + lnb_ref[...])
    y = jnp.dot(o_ref[...].astype(BF16), wtop_ref[...], preferred_element_type=F32)
    y = y + jnp.dot(cn.astype(BF16), wbot_ref[...], preferred_element_type=F32)
    out_ref[...] = res_ref[...] + _rms(y, gpost_ref[...])


def _even_out(o, c, lng, lnb, wtop, wbot, gpost, res):
    rows, d = res.shape
    da, dc = o.shape[1], c.shape[1]
    return pl.pallas_call(
        _even_out_body,
        grid=(rows // TM,),
        in_specs=[pl.BlockSpec((TM, da), lambda i: (i, 0)),
                  pl.BlockSpec((TM, dc), lambda i: (i, 0)),
                  pl.BlockSpec((1, dc), lambda i: (0, 0)),
                  pl.BlockSpec((1, dc), lambda i: (0, 0)),
                  pl.BlockSpec((da, d), lambda i: (0, 0)),
                  pl.BlockSpec((dc, d), lambda i: (0, 0)),
                  pl.BlockSpec((1, d), lambda i: (0, 0)),
                  pl.BlockSpec((TM, d), lambda i: (i, 0))],
        out_specs=pl.BlockSpec((TM, d), lambda i: (i, 0)),
        out_shape=jax.ShapeDtypeStruct((rows, d), F32),
        compiler_params=_cparams(("parallel",)),
        name="even_out",
    )(o, c, lng.reshape(1, dc), lnb.reshape(1, dc), wtop, wbot, gpost.reshape(1, d), res)


def _lam_from(lq_ref, lk_ref, lam_init):
    s0 = jnp.sum(lq_ref[0:1, :] * lk_ref[0:1, :], axis=-1, keepdims=True)
    s1 = jnp.sum(lq_ref[1:2, :] * lk_ref[1:2, :], axis=-1, keepdims=True)
    return jnp.exp(s0) - jnp.exp(s1) + lam_init


def _softmax_step(carry, q2, kc, vc, bias):
    m, l, acc = carry
    s = lax.dot_general(q2, kc, (((1,), (1,)), ((), ())), preferred_element_type=F32) + bias
    m_new = jnp.maximum(m, jnp.max(s, axis=-1, keepdims=True))
    alpha = jnp.exp(m - m_new)
    p = jnp.exp(s - m_new)
    l = alpha * l + jnp.sum(p, axis=-1, keepdims=True)
    acc = alpha * acc + jnp.dot(p.astype(BF16), vc, preferred_element_type=F32)
    return m_new, l, acc


def _stack_maps(q, scale):
    lane = lax.broadcasted_iota(jnp.int32, q.shape, 1)
    qs = q * scale
    q0 = jnp.where(lane < DK, qs, 0.0)
    q1 = jnp.where(lane >= DK, qs, 0.0)
    return jnp.concatenate([q0, q1], axis=0).astype(BF16)


def _diff_out(m, l, acc, n, lam, sg, lam_init):
    o = acc[:n] / l[:n] - lam * (acc[n:] / l[n:])
    return _rms(o, sg) * (1.0 - lam_init)


def _prompt_attn_body(slope_ref, lq_ref, lk_ref, sg_ref, q_ref, k_ref, v_ref, o_ref, kb, vb,
                      *, lam_init):
    slope = slope_ref[0, 0:1, 0:1]
    lam = _lam_from(lq_ref, lk_ref, lam_init)
    sg = sg_ref[...]
    scale = DK ** -0.5
    kb[...] = k_ref[...].astype(BF16)
    vb[...] = v_ref[...].astype(BF16)
    neg_inf = float("-inf")

    zpad = jnp.zeros((LANES - N_META, LANES), BF16)
    k_meta = jnp.concatenate([kb[0:N_META, :], zpad], axis=0)
    v_meta = jnp.concatenate([vb[0:N_META, :], zpad], axis=0)

    rm = lax.broadcasted_iota(jnp.int32, (2 * N_META, LANES), 0) % N_META
    cm = lax.broadcasted_iota(jnp.int32, (2 * N_META, LANES), 1)
    bias_m = jnp.where(cm <= rm, -slope * (rm - cm).astype(F32), neg_inf)
    q2 = _stack_maps(q_ref[0:N_META, :], scale)
    init = (jnp.full((2 * N_META, 1), neg_inf, F32), jnp.zeros((2 * N_META, 1), F32),
            jnp.zeros((2 * N_META, LANES), F32))
    m, l, acc = _softmax_step(init, q2, k_meta, v_meta, bias_m)
    o_ref[0:N_META, :] = _diff_out(m, l, acc, N_META, lam, sg, lam_init)

    r = lax.broadcasted_iota(jnp.int32, (2 * ATT_T, ATT_T), 0) % ATT_T
    c = lax.broadcasted_iota(jnp.int32, (2 * ATT_T, ATT_T), 1)
    base = slope * (r - c).astype(F32)
    bias_diag = jnp.where(c <= r, -base, neg_inf)
    rmeta = lax.broadcasted_iota(jnp.int32, (2 * ATT_T, LANES), 0) % ATT_T
    cmeta = lax.broadcasted_iota(jnp.int32, (2 * ATT_T, LANES), 1)
    base_meta = slope * (rmeta - cmeta + N_META).astype(F32)
    meta_ok = cmeta < N_META

    def q_tile(i, _):
        q0 = pl.multiple_of(N_META + i * ATT_T, 16)
        q2 = _stack_maps(q_ref[pl.ds(q0, ATT_T), :], scale)
        off = (i * ATT_T).astype(F32)
        bias_meta = jnp.where(meta_ok, -(base_meta + slope * off), neg_inf)
        carry = (jnp.full((2 * ATT_T, 1), neg_inf, F32), jnp.zeros((2 * ATT_T, 1), F32),
                 jnp.zeros((2 * ATT_T, LANES), F32))
        carry = _softmax_step(carry, q2, k_meta, v_meta, bias_meta)

        def k_tile(j, carry):
            k0 = pl.multiple_of(N_META + j * ATT_T, 16)
            d = ((i - j) * ATT_T).astype(F32)
            return _softmax_step(carry, q2, kb[pl.ds(k0, ATT_T), :], vb[pl.ds(k0, ATT_T), :],
                                 -(base + slope * d))

        carry = lax.fori_loop(0, i, k_tile, carry)
        m, l, acc = _softmax_step(carry, q2, kb[pl.ds(q0, ATT_T), :], vb[pl.ds(q0, ATT_T), :],
                                  bias_diag)
        o_ref[pl.ds(q0, ATT_T), :] = _diff_out(m, l, acc, ATT_T, lam, sg, lam_init)
        return 0

    lax.fori_loop(0, ATT_NT, q_tile, 0)


def _prompt_attn(q_all, k_all, v_all, slopes, lq, lk, sg, lam_init):
    blk = pl.BlockSpec((T_P, DV), lambda b, h: (b, h))
    return pl.pallas_call(
        functools.partial(_prompt_attn_body, lam_init=lam_init),
        grid=(BATCH, N_DH),
        in_specs=[pl.BlockSpec((1, 1, LANES), lambda b, h: (h, 0, 0)),
                  pl.BlockSpec((2, DK), lambda b, h: (0, 0)),
                  pl.BlockSpec((2, DK), lambda b, h: (0, 0)),
                  pl.BlockSpec((1, DV), lambda b, h: (0, 0)),
                  blk, blk, blk],
        out_specs=blk,
        out_shape=jax.ShapeDtypeStruct((ROWS, D_ATT), F32),
        scratch_shapes=[pltpu.VMEM((T_P, DV), BF16), pltpu.VMEM((T_P, DV), BF16)],
        compiler_params=_cparams(("parallel", "parallel")),
        name="prompt_attn",
    )(slopes, lq, lk, sg.reshape(1, DV), q_all, k_all, v_all)


def _decode_attn_body(pt_ref, srow_ref, qi_ref, lq_ref, lk_ref, sg_ref, q_ref, kn_ref, vn_ref, *refs,
                      lam_init, n_pages):
    k_refs, v_refs = refs[:DEC_PP], refs[DEC_PP:2 * DEC_PP]
    o_ref, m_s, l_s, acc_s = refs[2 * DEC_PP:]
    s_id = pl.program_id(1)
    n_rows = 2 * N_DH * DEC_SEQ
    past_len = n_pages * PAGE_SIZE
    neg_inf = float("-inf")

    row = lax.broadcasted_iota(jnp.int32, (n_rows, D_ATT), 0)
    lane = lax.broadcasted_iota(jnp.int32, (n_rows, D_ATT), 1)
    q2 = jnp.where(lane // DK == row // DEC_SEQ, q_ref[0] * (DK ** -0.5), 0.0).astype(BF16)
    srow = srow_ref[...]
    qi = qi_ref[...]
    col = lax.broadcasted_iota(jnp.int32, (n_rows, PAGE_SIZE), 1).astype(F32)

    @pl.when(s_id == 0)
    def _():
        m_s[...] = jnp.full(m_s.shape, neg_inf, F32)
        l_s[...] = jnp.zeros(l_s.shape, F32)
        acc_s[...] = jnp.zeros(acc_s.shape, F32)

    def step(kc, vc, bias):
        m, l, acc = _softmax_step((m_s[:, 0:1], l_s[:, 0:1], acc_s[...]), q2, kc, vc, bias)
        m_s[...] = jnp.broadcast_to(m, m_s.shape)
        l_s[...] = jnp.broadcast_to(l, l_s.shape)
        acc_s[...] = acc

    for p in range(DEC_PP):
        kpos0 = ((s_id * DEC_PP + p) * PAGE_SIZE).astype(F32)
        dist = (past_len + qi) - (kpos0 + col)
        step(k_refs[p][0].astype(BF16), v_refs[p][0].astype(BF16), -srow * dist)

    @pl.when(s_id == pl.num_programs(1) - 1)
    def _():
        dist = qi - col
        bias = jnp.where(dist >= 0, -srow * dist, neg_inf)
        step(kn_ref[0].astype(BF16), vn_ref[0].astype(BF16), bias)
        lam = _lam_from(lq_ref, lk_ref, lam_init)
        sg = sg_ref[...]
        l = l_s[:, 0:1]
        for h in range(N_DH):
            blk = acc_s[h * 2 * DEC_SEQ:(h + 1) * 2 * DEC_SEQ, h * DV:(h + 1) * DV]
            blk = blk / l[h * 2 * DEC_SEQ:(h + 1) * 2 * DEC_SEQ]
            o = blk[0:DEC_SEQ] - lam * blk[DEC_SEQ:2 * DEC_SEQ]
            o_ref[0, :, h * DV:(h + 1) * DV] = _rms(o, sg) * (1.0 - lam_init)


def _decode_attn(page_table, q_s, k_s, v_s, cache_k, cache_v, slopes_vec, lq, lk, sg, lam_init):
    n_pages = page_table.shape[1]
    n_rows = 2 * N_DH * DEC_SEQ
    q_t = jnp.tile(q_s, (1, 2 * N_DH, 1))
    pad = ((0, 0), (0, PAGE_SIZE - DEC_SEQ), (0, 0))
    kn = jnp.pad(k_s, pad)
    vn = jnp.pad(v_s, pad)
    ridx = jnp.arange(n_rows)
    srow = jnp.broadcast_to(slopes_vec[ridx // (2 * DEC_SEQ)][:, None], (n_rows, PAGE_SIZE)).astype(F32)
    qi = jnp.broadcast_to((ridx % DEC_SEQ)[:, None], (n_rows, PAGE_SIZE)).astype(F32)

    def page_spec(p):
        return pl.BlockSpec((1, PAGE_SIZE, D_ATT), lambda b, s, pt: (pt[b, s * DEC_PP + p], 0, 0))

    const2 = lambda b, s, pt: (0, 0)
    per_b = lambda b, s, pt: (b, 0, 0)
    grid_spec = pltpu.PrefetchScalarGridSpec(
        num_scalar_prefetch=1,
        grid=(DEC_BATCH, n_pages // DEC_PP),
        in_specs=[pl.BlockSpec((n_rows, PAGE_SIZE), const2),
                  pl.BlockSpec((n_rows, PAGE_SIZE), const2),
                  pl.BlockSpec((2, DK), const2),
                  pl.BlockSpec((2, DK), const2),
                  pl.BlockSpec((1, DV), const2),
                  pl.BlockSpec((1, n_rows, D_ATT), per_b),
                  pl.BlockSpec((1, PAGE_SIZE, D_ATT), per_b),
                  pl.BlockSpec((1, PAGE_SIZE, D_ATT), per_b)]
                 + [page_spec(p) for p in range(DEC_PP)]
                 + [page_spec(p) for p in range(DEC_PP)],
        out_specs=pl.BlockSpec((1, DEC_SEQ, D_ATT), per_b),
        scratch_shapes=[pltpu.VMEM((n_rows, LANES), F32), pltpu.VMEM((n_rows, LANES), F32),
                        pltpu.VMEM((n_rows, D_ATT), F32)],
    )
    return pl.pallas_call(
        functools.partial(_decode_attn_body, lam_init=lam_init, n_pages=n_pages),
        grid_spec=grid_spec,
        out_shape=jax.ShapeDtypeStruct((DEC_BATCH, DEC_SEQ, D_ATT), F32),
        compiler_params=_cparams(("parallel", "arbitrary")),
        name="decode_attn",
    )(page_table, srow, qi, lq, lk, sg.reshape(1, DV), q_t, kn, vn,
      *([cache_k] * DEC_PP), *([cache_v] * DEC_PP))


CONV_PRE = 32
CONV_RC = 48
CONV_CT = 256
assert T_P % CONV_RC == 0


def _conv_prompt_body(prev_ref, g_ref, w_ref, b_ref, o_ref, gp):
    gp[0:CONV_PRE, :] = prev_ref[0]
    gp[CONV_PRE:, :] = g_ref[...]
    lead = CONV_PRE - (CONV_W - 1)
    win_rows = CONV_RC + CONV_PRE
    bias = b_ref[...]

    def chunk(ci, _):
        t0 = pl.multiple_of(ci * CONV_RC, SUBLANES)
        win = gp[pl.ds(t0, win_rows), :]
        acc = jnp.broadcast_to(bias, (CONV_RC, CONV_CT))
        for r in range(SUBLANES):
            taps = [k for k in range(CONV_W) if (k + lead) % SUBLANES == r]
            if not taps:
                continue
            hi = max(k + lead for k in taps) - r + CONV_RC
            wr = win[r:r + hi, :]
            for k in taps:
                a = k + lead - r
                acc = acc + w_ref[k:k + 1, :] * wr[a:a + CONV_RC, :]
        o_ref[pl.ds(t0, CONV_RC), :] = acc
        return 0

    lax.fori_loop(0, T_P // CONV_RC, chunk, 0)


def _conv_prompt(g_all, prev, w, b):
    return pl.pallas_call(
        _conv_prompt_body,
        grid=(BATCH, D_CONV // CONV_CT),
        in_specs=[pl.BlockSpec((1, CONV_PRE, CONV_CT), lambda bi, j: (bi, 0, j)),
                  pl.BlockSpec((T_P, CONV_CT), lambda bi, j: (bi, j)),
                  pl.BlockSpec((CONV_W, CONV_CT), lambda bi, j: (0, j)),
                  pl.BlockSpec((1, CONV_CT), lambda bi, j: (0, j))],
        out_specs=pl.BlockSpec((T_P, CONV_CT), lambda bi, j: (bi, j)),
        out_shape=jax.ShapeDtypeStruct((ROWS, D_CONV), F32),
        scratch_shapes=[pltpu.VMEM((CONV_PRE + T_P, CONV_CT), F32)],
        compiler_params=_cparams(("parallel", "parallel")),
        name="conv_prompt",
    )(prev, g_all, w, b.reshape(1, D_CONV))


def _conv_sample_body(ext_ref, w_ref, b_ref, o_ref):
    w = w_ref[...]
    for bi in range(DEC_BATCH):
        for t in range(DEC_SEQ):
            acc = jnp.sum(w * ext_ref[bi, t:t + CONV_W, :], axis=0, keepdims=True) + b_ref[...]
            o_ref[bi * DEC_SEQ + t:bi * DEC_SEQ + t + 1, :] = acc


def _conv_sample(ext, w, b):
    return pl.pallas_call(
        _conv_sample_body,
        out_shape=jax.ShapeDtypeStruct((ROWS_S, D_CONV), F32),
        name="conv_sample",
    )(ext, w, b.reshape(1, D_CONV))


SSM_NK = SSM_L + 1


def _ssm_param_body(lre_ref, lim_ref, ldt_ref, bre_ref, bim_ref, cre_ref, cim_ref,
                    ap_re, ap_im, bp_re, bp_im, cp_re, cp_im, km_ref,
                    ar_s, ai_s, pr_s, pi_s, bbr_s, bbi_s):
    k = pl.program_id(0)

    @pl.when(k == 0)
    def _():
        lr, li = lre_ref[...], lim_ref[...]
        dt = jnp.exp(ldt_ref[...])
        er = jnp.exp(lr * dt)
        ar = er * jnp.cos(li * dt)
        ai = er * jnp.sin(li * dt)
        den = lr * lr + li * li
        xr, xi = ar - 1.0, ai
        cr = (xr * lr + xi * li) / den
        ci = (xi * lr - xr * li) / den
        br, bi = bre_ref[...], bim_ref[...]
        bbr_s[...] = cr * br - ci * bi
        bbi_s[...] = cr * bi + ci * br
        ar_s[...] = ar
        ai_s[...] = ai
        pr_s[...] = jnp.ones(pr_s.shape, F32)
        pi_s[...] = jnp.zeros(pi_s.shape, F32)

    @pl.when(k > 0)
    def _():
        pr, pi = pr_s[...], pi_s[...]
        ar, ai = ar_s[...], ai_s[...]
        pr_s[...] = pr * ar - pi * ai
        pi_s[...] = pr * ai + pi * ar

    pr, pi = pr_s[...], pi_s[...]
    ap_re[0] = pr
    ap_im[0] = pi
    bbr, bbi = bbr_s[...], bbi_s[...]
    bpr = pr * bbr - pi * bbi
    bpi = pr * bbi + pi * bbr
    bp_re[0] = bpr
    bp_im[0] = bpi
    cr, ci = cre_ref[...], cim_ref[...]
    cp_re[0] = pr * cr - pi * ci
    cp_im[0] = pr * ci + pi * cr
    dn = (((2,), (2,)), ((0,), (0,)))
    hp = lax.Precision.HIGHEST
    km_ref[0] = (lax.dot_general(cr, bpr, dn, precision=hp, preferred_element_type=F32)
                 - lax.dot_general(ci, bpi, dn, precision=hp, preferred_element_type=F32))


def _ssm_params(lre, lim, log_dt, b_re, b_im, c_re, c_im):
    g, p, c = N_SSM_GROUPS, SSM_P, SSM_GROUP
    bc = lambda a: jnp.broadcast_to(a[:, None, :], (g, c, p))
    ldt = jnp.broadcast_to(log_dt[:, None, None], (g, c, p))
    bt_re = jnp.swapaxes(b_re, 1, 2)
    bt_im = jnp.swapaxes(b_im, 1, 2)
    gcp = pl.BlockSpec((g, c, p), lambda k: (0, 0, 0))
    o_gcp = pl.BlockSpec((1, g, c, p), lambda k: (k, 0, 0, 0))
    outs = pl.pallas_call(
        _ssm_param_body,
        grid=(SSM_NK,),
        in_specs=[gcp] * 7,
        out_specs=[o_gcp] * 6 + [pl.BlockSpec((1, g, c, c), lambda k: (k, 0, 0, 0))],
        out_shape=[jax.ShapeDtypeStruct((SSM_NK, g, c, p), F32)] * 6
                  + [jax.ShapeDtypeStruct((SSM_NK, g, c, c), F32)],
        scratch_shapes=[pltpu.VMEM((g, c, p), F32)] * 6,
        compiler_params=_cparams(("arbitrary",)),
        name="ssm_params",
    )(bc(lre), bc(lim), ldt, bt_re, bt_im, c_re, c_im)
    ap_re, ap_im = outs[0][:, :, 0, :], outs[1][:, :, 0, :]
    return (ap_re, ap_im) + tuple(outs[2:])


def _ssm_expand(km, bp_re, bp_im, cp_re, cp_im, length):
    L, nj, g8, c, p = length, SSM_NJ, SSM_G8, SSM_GROUP, SSM_P
    eye = jnp.eye(g8, dtype=F32)
    s = jnp.arange(L)
    lag = s[None, :] - s[:, None]
    kst = jnp.where((lag >= 0)[:, :, None, None, None], km[jnp.clip(lag, 0, L - 1)], 0.0)
    kst = kst.reshape(L, L, nj, g8, c, c)
    kst = jnp.transpose(kst, (2, 0, 3, 5, 1, 4))
    toep = kst[:, :, :, :, :, None, :] * eye[None, None, :, None, None, :, None]
    toep = toep.reshape(nj, L * LANES, L * LANES).astype(BF16)

    bsel = jnp.stack([bp_re[L - 1 - s], bp_im[L - 1 - s]], axis=0)
    bsel = bsel.reshape(2, L, nj, g8, c, p)
    bsel = jnp.transpose(bsel, (2, 1, 3, 4, 0, 5))
    ws = bsel[:, :, :, :, :, None, :] * eye[None, None, :, None, None, :, None]
    ws = ws.reshape(nj, L * LANES, 2 * SSM_SW).astype(BF16)

    csel = jnp.stack([cp_re[1 + s], -cp_im[1 + s]], axis=0)
    csel = csel.reshape(2, L, nj, g8, c, p)
    csel = jnp.transpose(csel, (2, 0, 3, 5, 1, 4))
    wc = csel[:, :, :, :, :, None, :] * eye[None, None, :, None, None, :, None]
    wc = wc.reshape(nj, 2 * SSM_SW, L * LANES).astype(BF16)
    return toep, ws, wc


def _gather_chunks(u_ref, length, m):
    xs = [u_ref[pl.ds(s, m, stride=length), :].astype(BF16) for s in range(length)]
    return jnp.concatenate(xs, axis=1)


def _ssm_intra_body(u_ref, w_ref, o_ref):
    o_ref[0] = jnp.dot(_gather_chunks(u_ref, SSM_L, SSM_M), w_ref[0], preferred_element_type=F32)


def _ssm_intra(u_all, toep):
    n = SSM_L * LANES
    return pl.pallas_call(
        _ssm_intra_body,
        grid=(SSM_NJ,),
        in_specs=[pl.BlockSpec((ROWS, LANES), lambda j: (0, j)),
                  pl.BlockSpec((1, n, n), lambda j: (j, 0, 0))],
        out_specs=pl.BlockSpec((1, SSM_M, n), lambda j: (j, 0, 0)),
        out_shape=jax.ShapeDtypeStruct((SSM_NJ, SSM_M, n), F32),
        compiler_params=_cparams(("parallel",)),
        name="ssm_intra",
    )(u_all, toep)


SSM_SQ = 2 * SSM_SW // LANES


def _ssm_state_body(u_ref, w_ref, o_ref):
    s = jnp.dot(_gather_chunks(u_ref, SSM_L, SSM_M), w_ref[0], preferred_element_type=F32)
    for q in range(SSM_SQ):
        o_ref[0, q] = s[:, q * LANES:(q + 1) * LANES]


def _ssm_state(u_all, ws):
    return pl.pallas_call(
        _ssm_state_body,
        grid=(SSM_NJ,),
        in_specs=[pl.BlockSpec((ROWS, LANES), lambda j: (0, j)),
                  pl.BlockSpec((1, SSM_L * LANES, 2 * SSM_SW), lambda j: (j, 0, 0))],
        out_specs=pl.BlockSpec((1, SSM_SQ, SSM_M, LANES), lambda j: (j, 0, 0, 0)),
        out_shape=jax.ShapeDtypeStruct((SSM_NJ, SSM_SQ, SSM_M, LANES), F32),
        compiler_params=_cparams(("parallel",)),
        name="ssm_state",
    )(u_all, ws)


def _ssm_scan_body(s_ref, are_ref, aim_ref, hp_ref, fre_ref, fim_ref):
    nq = SSM_SQ // 2
    ar = [are_ref[:, q * LANES:(q + 1) * LANES] for q in range(nq)]
    ai = [aim_ref[:, q * LANES:(q + 1) * LANES] for q in range(nq)]
    hp_ref[...] = jnp.zeros(hp_ref.shape, F32)

    def body(c, carry):
        rows = pl.ds(c, BATCH, stride=SSM_NC)
        out = []
        for q in range(nq):
            hr, hi = carry[2 * q], carry[2 * q + 1]
            hp_ref[0, q, rows, :] = hr
            hp_ref[0, nq + q, rows, :] = hi
            sr, si = s_ref[0, q, rows, :], s_ref[0, nq + q, rows, :]
            out.append(ar[q] * hr - ai[q] * hi + sr)
            out.append(ar[q] * hi + ai[q] * hr + si)
        return tuple(out)

    z = jnp.zeros((BATCH, LANES), F32)
    fin = lax.fori_loop(0, SSM_NC, body, (z,) * SSM_SQ)
    for q in range(nq):
        fre_ref[:, q * LANES:(q + 1) * LANES] = fin[2 * q]
        fim_ref[:, q * LANES:(q + 1) * LANES] = fin[2 * q + 1]


def _ssm_scan(s_all, a_re, a_im):
    st = pl.BlockSpec((1, SSM_SQ, SSM_M, LANES), lambda j: (j, 0, 0, 0))
    av = pl.BlockSpec((1, SSM_SW), lambda j: (0, j))
    fin = pl.BlockSpec((BATCH, SSM_SW), lambda j: (0, j))
    gp = N_SSM_GROUPS * SSM_P
    return pl.pallas_call(
        _ssm_scan_body,
        grid=(SSM_NJ,),
        in_specs=[st, av, av],
        out_specs=[st, fin, fin],
        out_shape=[jax.ShapeDtypeStruct((SSM_NJ, SSM_SQ, SSM_M, LANES), F32),
                   jax.ShapeDtypeStruct((BATCH, gp), F32),
                   jax.ShapeDtypeStruct((BATCH, gp), F32)],
        compiler_params=_cparams(("parallel",)),
        name="ssm_scan",
    )(s_all, a_re, a_im)


def _ssm_comb_body(u_ref, yi_ref, hp_ref, wc_ref, d_ref, o_ref):
    hp = jnp.concatenate([hp_ref[0, q].astype(BF16) for q in range(SSM_SQ)], axis=1)
    y = yi_ref[0] + jnp.dot(hp, wc_ref[0], preferred_element_type=F32)
    d = d_ref[...]
    for t in range(SSM_L):
        rows = pl.ds(t, SSM_M, stride=SSM_L)
        o_ref[rows, :] = y[:, t * LANES:(t + 1) * LANES] + d * u_ref[rows, :]


def _ssm_comb(u_all, y_intra, h_prev, wc, dvec):
    return pl.pallas_call(
        _ssm_comb_body,
        grid=(SSM_NJ,),
        in_specs=[pl.BlockSpec((ROWS, LANES), lambda j: (0, j)),
                  pl.BlockSpec((1, SSM_M, SSM_L * LANES), lambda j: (j, 0, 0)),
                  pl.BlockSpec((1, SSM_SQ, SSM_M, LANES), lambda j: (j, 0, 0, 0)),
                  pl.BlockSpec((1, 2 * SSM_SW, SSM_L * LANES), lambda j: (j, 0, 0)),
                  pl.BlockSpec((1, LANES), lambda j: (0, j))],
        out_specs=pl.BlockSpec((ROWS, LANES), lambda j: (0, j)),
        out_shape=jax.ShapeDtypeStruct((ROWS, D_MODEL), F32),
        compiler_params=_cparams(("parallel",)),
        name="ssm_comb",
    )(u_all, y_intra, h_prev, wc, dvec)


def _ssm_sample_body(u_ref, toep_ref, ws_ref, wc_ref, hre_ref, him_ref, are_ref, aim_ref, d_ref,
                     y_any, o_ref, fre_ref, fim_ref):
    del y_any
    x = _gather_chunks(u_ref, DEC_SEQ, DEC_BATCH)
    hr, hi = hre_ref[...], him_ref[...]
    s = jnp.dot(x, ws_ref[0], preferred_element_type=F32)
    h = jnp.concatenate([hr, hi], axis=1).astype(BF16)
    y = (jnp.dot(x, toep_ref[0], preferred_element_type=F32)
         + jnp.dot(h, wc_ref[0], preferred_element_type=F32))
    ar, ai = are_ref[...], aim_ref[...]
    fre_ref[...] = ar * hr - ai * hi + s[:, 0:SSM_SW]
    fim_ref[...] = ar * hi + ai * hr + s[:, SSM_SW:2 * SSM_SW]
    d = d_ref[...]
    for t in range(DEC_SEQ):
        rows = pl.ds(t, DEC_BATCH, stride=DEC_SEQ)
        o_ref[rows, :] = y[:, t * LANES:(t + 1) * LANES] + d * u_ref[rows, :]


def _ssm_sample(u_all, y_all, toep, ws, wc, h_re, h_im, a_re, a_im, dvec):
    ls = DEC_SEQ * LANES
    gp = N_SSM_GROUPS * SSM_P
    st = pl.BlockSpec((DEC_BATCH, SSM_SW), lambda j: (0, j))
    av = pl.BlockSpec((1, SSM_SW), lambda j: (0, j))
    rows = pl.BlockSpec((ROWS_S, LANES), lambda j: (SAMPLE_BLK, j))
    return pl.pallas_call(
        _ssm_sample_body,
        grid=(SSM_NJ,),
        in_specs=[rows,
                  pl.BlockSpec((1, ls, ls), lambda j: (j, 0, 0)),
                  pl.BlockSpec((1, ls, 2 * SSM_SW), lambda j: (j, 0, 0)),
                  pl.BlockSpec((1, 2 * SSM_SW, ls), lambda j: (j, 0, 0)),
                  st, st, av, av,
                  pl.BlockSpec((1, LANES), lambda j: (0, j)),
                  pl.BlockSpec(memory_space=pl.ANY)],
        out_specs=[rows, st, st],
        out_shape=[jax.ShapeDtypeStruct((ROWS, D_MODEL), F32),
                   jax.ShapeDtypeStruct((DEC_BATCH, gp), F32),
                   jax.ShapeDtypeStruct((DEC_BATCH, gp), F32)],
        input_output_aliases={9: 0},
        compiler_params=_cparams(("parallel",)),
        name="ssm_sample",
    )(u_all, toep, ws, wc, h_re, h_im, a_re, a_im, dvec, y_all)


def kernel(x_prompt, x_sample, cache_k, cache_v, state_conv, state_ssm_re, state_ssm_im, page_table, meta_tokens, norm_mix_pre, norm_mix_post, norm_ffn_pre, norm_ffn_post, w_in_even, lambda_q, lambda_k, subln_g, conv_w, conv_b, conv_ln_g, conv_ln_b, w_out_even, w_in_odd, ssm_a_re, ssm_a_im, ssm_b_re, ssm_b_im, ssm_c_re, ssm_c_im, ssm_d, ssm_log_dt, w_glu, w_out_odd, w_ffn_gate, w_ffn_up, w_ffn_down):
    n_pool = cache_k.shape[1]
    meta = jnp.broadcast_to(meta_tokens.astype(F32)[None], (BATCH, N_META, D_MODEL))
    x = jnp.concatenate([
        jnp.concatenate([meta, x_prompt], axis=1).reshape(ROWS_P, D_MODEL),
        x_sample.reshape(ROWS_S, D_MODEL),
        jnp.zeros((ROWS - ROWS_P - ROWS_S, D_MODEL), F32)], axis=0)
    slopes_vec = jnp.array([2.0 ** (-8.0 * (i + 1) / N_DH) for i in range(N_DH)], dtype=F32)
    slopes = jnp.broadcast_to(slopes_vec[:, None, None], (N_DH, 1, LANES))
    sample = slice(ROWS_P, ROWS_P + ROWS_S)

    ks, vs, convs_p, convs_s, sre_p, sim_p, sre_s, sim_s = [], [], [], [], [], [], [], []
    for layer in range(DEPTH):
        if layer % 2 == 0:
            e = layer // 2
            lam_init = 0.8 - 0.6 * math.exp(-0.3 * (2 * e))
            w_in = w_in_even[e].astype(BF16)
            gpre = norm_mix_pre[layer]
            q_all = _norm_mm(x, gpre, [w_in[:, 0:D_ATT]])
            k_all = _norm_mm(x, gpre, [w_in[:, D_ATT:2 * D_ATT]])
            v_all = _norm_mm(x, gpre, [w_in[:, 2 * D_ATT:3 * D_ATT]])
            g_all = _norm_mm(x, gpre, [w_in[:, 3 * D_ATT:3 * D_ATT + D_CONV],
                                       w_in[:, 3 * D_ATT + D_CONV:]])
            o_all = _prompt_attn(q_all, k_all, v_all, slopes, lambda_q[e], lambda_k[e], subln_g[e],
                                 lam_init)
            o_s = _decode_attn(page_table,
                               q_all[sample].reshape(DEC_BATCH, DEC_SEQ, D_ATT),
                               k_all[sample].reshape(DEC_BATCH, DEC_SEQ, D_ATT),
                               v_all[sample].reshape(DEC_BATCH, DEC_SEQ, D_ATT),
                               cache_k[e].reshape(n_pool, PAGE_SIZE, D_ATT),
                               cache_v[e].reshape(n_pool, PAGE_SIZE, D_ATT),
                               slopes_vec, lambda_q[e], lambda_k[e], subln_g[e], lam_init)
            tail = jnp.zeros((ROWS - ROWS_P - ROWS_S, D_ATT), F32)
            o_all = lax.dynamic_update_slice(
                o_all, jnp.concatenate([o_s.reshape(ROWS_S, D_ATT), tail], axis=0), (ROWS_P, 0))
            g_s = g_all[sample].reshape(DEC_BATCH, DEC_SEQ, D_CONV)
            ext_s = jnp.concatenate([state_conv[e], g_s], axis=1)
            c_all = _conv_prompt(g_all, jnp.zeros((BATCH, CONV_PRE, D_CONV), F32), conv_w[e], conv_b[e])
            c_s = _conv_sample(ext_s, conv_w[e], conv_b[e])
            c_all = lax.dynamic_update_slice(c_all, jnp.concatenate([c_s, tail], axis=0), (ROWS_P, 0))
            w_out = w_out_even[e].astype(BF16)
            x = _even_out(o_all, c_all, conv_ln_g[e], conv_ln_b[e], w_out[:D_ATT], w_out[D_ATT:],
                          norm_mix_post[layer], x)
            ks.append((k_all[:ROWS_P].reshape(BATCH, T_P, N_DH, 2, DK),
                       k_all[sample].reshape(DEC_BATCH, DEC_SEQ, N_DH, 2, DK)))
            vs.append((v_all[:ROWS_P].reshape(BATCH, T_P, N_DH, DV),
                       v_all[sample].reshape(DEC_BATCH, DEC_SEQ, N_DH, DV)))
            convs_p.append(g_all[:ROWS_P].reshape(BATCH, T_P, D_CONV)[:, T_P - (CONV_W - 1):])
            convs_s.append(ext_s[:, DEC_SEQ:])
        else:
            o = layer // 2
            gp = N_SSM_GROUPS * SSM_P
            u_all = _norm_mm(x, norm_mix_pre[layer], [w_in_odd[o].astype(BF16)])
            ap_re, ap_im, bp_re, bp_im, cp_re, cp_im, km = _ssm_params(
                ssm_a_re[o], ssm_a_im[o], ssm_log_dt[o], ssm_b_re[o], ssm_b_im[o],
                ssm_c_re[o], ssm_c_im[o])
            dvec = ssm_d[o].reshape(1, D_MODEL)
            toep, ws, wc = _ssm_expand(km, bp_re, bp_im, cp_re, cp_im, SSM_L)
            y_intra = _ssm_intra(u_all, toep)
            s_loc = _ssm_state(u_all, ws)
            h_prev, f_re, f_im = _ssm_scan(s_loc, ap_re[SSM_L].reshape(1, gp), ap_im[SSM_L].reshape(1, gp))
            y_all = _ssm_comb(u_all, y_intra, h_prev, wc, dvec)
            toep_s, ws_s, wc_s = _ssm_expand(km, bp_re, bp_im, cp_re, cp_im, DEC_SEQ)
            y_all, g_re, g_im = _ssm_sample(
                u_all, y_all, toep_s, ws_s, wc_s,
                state_ssm_re[o].reshape(DEC_BATCH, gp), state_ssm_im[o].reshape(DEC_BATCH, gp),
                ap_re[DEC_SEQ].reshape(1, gp), ap_im[DEC_SEQ].reshape(1, gp), dvec)
            x = _odd_out(y_all, x, w_glu[o].astype(BF16), w_out_odd[o].astype(BF16),
                         norm_mix_post[layer])
            sre_p.append(f_re.reshape(BATCH, N_SSM_GROUPS, SSM_P))
            sim_p.append(f_im.reshape(BATCH, N_SSM_GROUPS, SSM_P))
            sre_s.append(g_re.reshape(DEC_BATCH, N_SSM_GROUPS, SSM_P))
            sim_s.append(g_im.reshape(DEC_BATCH, N_SSM_GROUPS, SSM_P))
        x = _ffn(x, norm_ffn_pre[layer], w_ffn_gate[layer].astype(BF16), w_ffn_up[layer].astype(BF16),
                 w_ffn_down[layer].astype(BF16), norm_ffn_post[layer])

    y_prompt = x[:ROWS_P].reshape(BATCH, T_P, D_MODEL)[:, N_META:]
    y_sample = x[sample].reshape(DEC_BATCH, DEC_SEQ, D_MODEL)
    return (y_prompt, y_sample,
            jnp.stack([k[0] for k in ks]), jnp.stack([v[0] for v in vs]),
            jnp.stack([k[1] for k in ks]), jnp.stack([v[1] for v in vs]),
            jnp.stack(convs_p), jnp.stack(convs_s),
            jnp.stack(sre_p), jnp.stack(sim_p), jnp.stack(sre_s), jnp.stack(sim_s))
```

```python
import functools
import math

import jax
import jax.numpy as jnp
from jax import lax
from jax.experimental import pallas as pl
from jax.experimental.pallas import tpu as pltpu

F32 = jnp.float32
BF16 = jnp.bfloat16

D_MODEL = 2048
BATCH = 4
SEQ = 2048
DEPTH = 2
DEC_BATCH = 8
DEC_SEQ = 4
PAGE_SIZE = 128
N_META = 16
D_ATT = D_MODEL // 2
N_DH = 8
DK = D_ATT // N_DH // 2
DV = 2 * DK
D_CONV = D_MODEL - D_ATT
CONV_W = 31
SSM_GROUP = 16
N_SSM_GROUPS = D_MODEL // SSM_GROUP
SSM_P = 64
D_FF = ((8 * D_MODEL // 3 + 255) // 256) * 256
EPS = 1e-6

T_P = N_META + SEQ
ROWS_P = BATCH * T_P
ROWS_S = DEC_BATCH * DEC_SEQ
TM = 640
ROWS = 8320
assert ROWS % TM == 0 and ROWS >= ROWS_P + ROWS_S
assert ROWS_P % ROWS_S == 0
SAMPLE_BLK = ROWS_P // ROWS_S

LANES = 128
SUBLANES = 8
VMEM_LIMIT = 56 * 1024 * 1024

SSM_L = 16
SSM_NC = T_P // SSM_L
SSM_M = ROWS // SSM_L
SSM_NJ = D_MODEL // LANES
SSM_G8 = LANES // SSM_GROUP
SSM_SW = SSM_G8 * SSM_P
assert T_P % SSM_L == 0 and ROWS % SSM_L == 0

ATT_T = 256
ATT_NT = SEQ // ATT_T
DEC_PP = 4


def _cparams(sem, vmem=VMEM_LIMIT):
    return pltpu.CompilerParams(dimension_semantics=sem, vmem_limit_bytes=vmem)


def _rms(x, g):
    ms = jnp.mean(x * x, axis=-1, keepdims=True)
    return x * lax.rsqrt(ms + EPS) * g


def _norm_mm_body(x_ref, g_ref, *refs, glu):
    n_w = 2 if glu else 1
    w_refs, o_ref, hb = refs[:n_w], refs[n_w], refs[n_w + 1]

    @pl.when(pl.program_id(1) == 0)
    def _():
        hb[...] = _rms(x_ref[...], g_ref[...]).astype(BF16)

    h = hb[...]
    a = jnp.dot(h, w_refs[0][...], preferred_element_type=F32)
    if glu:
        b = jnp.dot(h, w_refs[1][...], preferred_element_type=F32)
        a = a * jax.nn.sigmoid(b)
    o_ref[...] = a


def _norm_mm(x, g, ws, *, tn=512):
    rows, d = x.shape
    n = ws[0].shape[1]
    glu = len(ws) == 2
    return pl.pallas_call(
        functools.partial(_norm_mm_body, glu=glu),
        grid=(rows // TM, n // tn),
        in_specs=[pl.BlockSpec((TM, d), lambda i, j: (i, 0)),
                  pl.BlockSpec((1, d), lambda i, j: (0, 0))]
                 + [pl.BlockSpec((d, tn), lambda i, j: (0, j)) for _ in ws],
        out_specs=pl.BlockSpec((TM, tn), lambda i, j: (i, j)),
        out_shape=jax.ShapeDtypeStruct((rows, n), F32),
        scratch_shapes=[pltpu.VMEM((TM, d), BF16)],
        compiler_params=_cparams(("parallel", "arbitrary")),
        name="norm_mm_glu" if glu else "norm_mm",
    )(x, g.reshape(1, d), *ws)


def _ffn_body(x_ref, gpre_ref, wg_ref, wu_ref, wd_ref, gpost_ref, o_ref, hb, acc):
    j = pl.program_id(1)

    @pl.when(j == 0)
    def _():
        hb[...] = _rms(x_ref[...], gpre_ref[...]).astype(BF16)

    h = hb[...]
    gate = jnp.dot(h, wg_ref[...], preferred_element_type=F32)
    up = jnp.dot(h, wu_ref[...], preferred_element_type=F32)
    a = (jax.nn.silu(gate) * up).astype(BF16)
    part = jnp.dot(a, wd_ref[...], preferred_element_type=F32)

    @pl.when(j == 0)
    def _():
        acc[...] = part

    @pl.when(j > 0)
    def _():
        acc[...] += part

    @pl.when(j == pl.num_programs(1) - 1)
    def _():
        o_ref[...] = x_ref[...] + _rms(acc[...], gpost_ref[...])


def _ffn(x, gpre, wg, wu, wd, gpost, *, tf=512):
    rows, d = x.shape
    dff = wg.shape[1]
    return pl.pallas_call(
        _ffn_body,
        grid=(rows // TM, dff // tf),
        in_specs=[pl.BlockSpec((TM, d), lambda i, j: (i, 0)),
                  pl.BlockSpec((1, d), lambda i, j: (0, 0)),
                  pl.BlockSpec((d, tf), lambda i, j: (0, j)),
                  pl.BlockSpec((d, tf), lambda i, j: (0, j)),
                  pl.BlockSpec((tf, d), lambda i, j: (j, 0)),
                  pl.BlockSpec((1, d), lambda i, j: (0, 0))],
        out_specs=pl.BlockSpec((TM, d), lambda i, j: (i, 0)),
        out_shape=jax.ShapeDtypeStruct((rows, d), F32),
        scratch_shapes=[pltpu.VMEM((TM, d), BF16), pltpu.VMEM((TM, d), F32)],
        compiler_params=_cparams(("parallel", "arbitrary")),
        name="ffn",
    )(x, gpre.reshape(1, d), wg, wu, wd, gpost.reshape(1, d))


def _odd_out_body(y_ref, res_ref, wglu_ref, wout_ref, gpost_ref, o_ref, yf, hb, acc, *, tf):
    j = pl.program_id(1)
    nj = pl.num_programs(1)

    @pl.when(j == 0)
    def _():
        for jj in range(yf.shape[0]):
            gy = jax.nn.gelu(y_ref[:, jj * tf:(jj + 1) * tf])
            yf[jj] = gy
            hb[:, jj * tf:(jj + 1) * tf] = gy.astype(BF16)

    t = jnp.dot(hb[...], wglu_ref[...], preferred_element_type=F32)
    a = (yf[j] * jax.nn.sigmoid(t)).astype(BF16)
    part = jnp.dot(a, wout_ref[...], preferred_element_type=F32)

    @pl.when(j == 0)
    def _():
        acc[...] = part

    @pl.when(j > 0)
    def _():
        acc[...] += part

    @pl.when(j == nj - 1)
    def _():
        o_ref[...] = res_ref[...] + _rms(acc[...], gpost_ref[...])


def _odd_out(y, res, wglu, wout, gpost, *, tf=512, tm=TM // 2):
    rows, d = y.shape
    return pl.pallas_call(
        functools.partial(_odd_out_body, tf=tf),
        grid=(rows // tm, d // tf),
        in_specs=[pl.BlockSpec((tm, d), lambda i, j: (i, 0)),
                  pl.BlockSpec((tm, d), lambda i, j: (i, 0)),
                  pl.BlockSpec((d, tf), lambda i, j: (0, j)),
                  pl.BlockSpec((tf, d), lambda i, j: (j, 0)),
                  pl.BlockSpec((1, d), lambda i, j: (0, 0))],
        out_specs=pl.BlockSpec((tm, d), lambda i, j: (i, 0)),
        out_shape=jax.ShapeDtypeStruct((rows, d), F32),
        scratch_shapes=[pltpu.VMEM((d // tf, tm, tf), F32), pltpu.VMEM((tm, d), BF16),
                        pltpu.VMEM((tm, d), F32)],
        compiler_params=_cparams(("parallel", "arbitrary")),
        name="odd_out",
    )(y, res, wglu, wout, gpost.reshape(1, d))


def _even_out_body(o_ref, c_ref, lng_ref, lnb_ref, wtop_ref, wbot_ref, gpost_ref, res_ref, out_ref):
    c = c_ref[...]
    mu = jnp.mean(c, axis=-1, keepdims=True)
    var = jnp.mean(jnp.square(c - mu), axis=-1, keepdims=True)
    cn = jax.nn.silu((c - mu) * lax.rsqrt(var + EPS) * lng_ref[...] + lnb_ref[...])
    y = jnp.dot(o_ref[...].astype(BF16), wtop_ref[...], preferred_element_type=F32)
    y = y + jnp.dot(cn.astype(BF16), wbot_ref[...], preferred_element_type=F32)
    out_ref[...] = res_ref[...] + _rms(y, gpost_ref[...])


def _even_out(o, c, lng, lnb, wtop, wbot, gpost, res):
    rows, d = res.shape
    da, dc = o.shape[1], c.shape[1]
    return pl.pallas_call(
        _even_out_body,
        grid=(rows // TM,),
        in_specs=[pl.BlockSpec((TM, da), lambda i: (i, 0)),
                  pl.BlockSpec((TM, dc), lambda i: (i, 0)),
                  pl.BlockSpec((1, dc), lambda i: (0, 0)),
                  pl.BlockSpec((1, dc), lambda i: (0, 0)),
                  pl.BlockSpec((da, d), lambda i: (0, 0)),
                  pl.BlockSpec((dc, d), lambda i: (0, 0)),
                  pl.BlockSpec((1, d), lambda i: (0, 0)),
                  pl.BlockSpec((TM, d), lambda i: (i, 0))],
        out_specs=pl.BlockSpec((TM, d), lambda i: (i, 0)),
        out_shape=jax.ShapeDtypeStruct((rows, d), F32),
        compiler_params=_cparams(("parallel",)),
        name="even_out",
    )(o, c, lng.reshape(1, dc), lnb.reshape(1, dc), wtop, wbot, gpost.reshape(1, d), res)


def _lam_from(lq_ref, lk_ref, lam_init):
    s0 = jnp.sum(lq_ref[0:1, :] * lk_ref[0:1, :], axis=-1, keepdims=True)
    s1 = jnp.sum(lq_ref[1:2, :] * lk_ref[1:2, :], axis=-1, keepdims=True)
    return jnp.exp(s0) - jnp.exp(s1) + lam_init


def _softmax_step(carry, q2, kc, vc, bias, k_is_transposed=False):
    m, l, acc = carry
    if k_is_transposed:
        s = jnp.dot(q2, kc, preferred_element_type=F32) + bias
    else:
        s = lax.dot_general(q2, kc, (((1,), (1,)), ((), ())), preferred_element_type=F32) + bias
    m_new = jnp.maximum(m, jnp.max(s, axis=-1, keepdims=True))
    alpha = jnp.exp(m - m_new)
    p = jnp.exp(s - m_new)
    l = alpha * l + jnp.sum(p, axis=-1, keepdims=True)
    acc = alpha * acc + jnp.dot(p.astype(BF16), vc, preferred_element_type=F32)
    return m_new, l, acc


def _stack_maps(q, scale):
    lane = lax.broadcasted_iota(jnp.int32, q.shape, 1)
    qs = q * scale
    q0 = jnp.where(lane < DK, qs, 0.0)
    q1 = jnp.where(lane >= DK, qs, 0.0)
    return jnp.concatenate([q0, q1], axis=0).astype(BF16)


def _diff_out(m, l, acc, n, lam, sg, lam_init):
    o = acc[:n] / l[:n] - lam * (acc[n:] / l[n:])
    return _rms(o, sg) * (1.0 - lam_init)


def _prompt_attn_body(slope_ref, lq_ref, lk_ref, sg_ref, q_ref, k_ref, v_ref, o_ref, kb, vb,
                      *, lam_init):
    slope = slope_ref[0, 0:1, 0:1]
    lam = _lam_from(lq_ref, lk_ref, lam_init)
    sg = sg_ref[...]
    scale = DK ** -0.5
    kb[...] = k_ref[...].astype(BF16)
    vb[...] = v_ref[...].astype(BF16)
    neg_inf = float("-inf")

    zpad = jnp.zeros((LANES - N_META, LANES), BF16)
    k_meta = jnp.concatenate([kb[0:N_META, :], zpad], axis=0)
    v_meta = jnp.concatenate([vb[0:N_META, :], zpad], axis=0)

    rm = lax.broadcasted_iota(jnp.int32, (2 * N_META, LANES), 0) % N_META
    cm = lax.broadcasted_iota(jnp.int32, (2 * N_META, LANES), 1)
    bias_m = jnp.where(cm <= rm, -slope * (rm - cm).astype(F32), neg_inf)
    q2 = _stack_maps(q_ref[0:N_META, :], scale)
    init = (jnp.full((2 * N_META, 1), neg_inf, F32), jnp.zeros((2 * N_META, 1), F32),
            jnp.zeros((2 * N_META, LANES), F32))
    m, l, acc = _softmax_step(init, q2, k_meta, v_meta, bias_m)
    o_ref[0:N_META, :] = _diff_out(m, l, acc, N_META, lam, sg, lam_init)

    r = lax.broadcasted_iota(jnp.int32, (2 * ATT_T, ATT_T), 0) % ATT_T
    c = lax.broadcasted_iota(jnp.int32, (2 * ATT_T, ATT_T), 1)
    base = slope * (r - c).astype(F32)
    bias_diag = jnp.where(c <= r, -base, neg_inf)
    rmeta = lax.broadcasted_iota(jnp.int32, (2 * ATT_T, LANES), 0) % ATT_T
    cmeta = lax.broadcasted_iota(jnp.int32, (2 * ATT_T, LANES), 1)
    base_meta = slope * (rmeta - cmeta + N_META).astype(F32)
    meta_ok = cmeta < N_META

    def q_tile(i, _):
        q0 = pl.multiple_of(N_META + i * ATT_T, 16)
        q2 = _stack_maps(q_ref[pl.ds(q0, ATT_T), :], scale)
        off = (i * ATT_T).astype(F32)
        bias_meta = jnp.where(meta_ok, -(base_meta + slope * off), neg_inf)
        carry = (jnp.full((2 * ATT_T, 1), neg_inf, F32), jnp.zeros((2 * ATT_T, 1), F32),
                 jnp.zeros((2 * ATT_T, LANES), F32))
        carry = _softmax_step(carry, q2, k_meta, v_meta, bias_meta)

        def k_tile(j, carry):
            k0 = pl.multiple_of(N_META + j * ATT_T, 16)
            d = ((i - j) * ATT_T).astype(F32)
            return _softmax_step(carry, q2, kb[pl.ds(k0, ATT_T), :], vb[pl.ds(k0, ATT_T), :],
                                 -(base + slope * d))

        carry = lax.fori_loop(0, i, k_tile, carry)
        m, l, acc = _softmax_step(carry, q2, kb[pl.ds(q0, ATT_T), :], vb[pl.ds(q0, ATT_T), :],
                                  bias_diag)
        o_ref[pl.ds(q0, ATT_T), :] = _diff_out(m, l, acc, ATT_T, lam, sg, lam_init)
        return 0

    lax.fori_loop(0, ATT_NT, q_tile, 0)


def _prompt_attn(q_all, k_all, v_all, slopes, lq, lk, sg, lam_init):
    blk = pl.BlockSpec((T_P, DV), lambda b, h: (b, h))
    return pl.pallas_call(
        functools.partial(_prompt_attn_body, lam_init=lam_init),
        grid=(BATCH, N_DH),
        in_specs=[pl.BlockSpec((1, 1, LANES), lambda b, h: (h, 0, 0)),
                  pl.BlockSpec((2, DK), lambda b, h: (0, 0)),
                  pl.BlockSpec((2, DK), lambda b, h: (0, 0)),
                  pl.BlockSpec((1, DV), lambda b, h: (0, 0)),
                  blk, blk, blk],
        out_specs=blk,
        out_shape=jax.ShapeDtypeStruct((ROWS, D_ATT), F32),
        scratch_shapes=[pltpu.VMEM((T_P, DV), BF16), pltpu.VMEM((T_P, DV), BF16)],
        compiler_params=_cparams(("parallel", "parallel")),
        name="prompt_attn",
    )(slopes, lq, lk, sg.reshape(1, DV), q_all, k_all, v_all)


def _decode_attn_body(pt_ref, srow_ref, qi_ref, lq_ref, lk_ref, sg_ref, q_ref, kn_ref, vn_ref, *refs,
                      lam_init, n_pages):
    k_refs, v_refs = refs[:DEC_PP], refs[DEC_PP:2 * DEC_PP]
    o_ref, m_s, l_s, acc_s = refs[2 * DEC_PP:]
    s_id = pl.program_id(1)
    n_rows = 2 * N_DH * DEC_SEQ
    past_len = n_pages * PAGE_SIZE
    neg_inf = float("-inf")

    row = lax.broadcasted_iota(jnp.int32, (n_rows, D_ATT), 0)
    lane = lax.broadcasted_iota(jnp.int32, (n_rows, D_ATT), 1)
    q2 = jnp.where(lane // DK == row // DEC_SEQ, q_ref[0] * (DK ** -0.5), 0.0).astype(BF16)
    srow = srow_ref[...]
    qi = qi_ref[...]
    col = lax.broadcasted_iota(jnp.int32, (n_rows, PAGE_SIZE), 1).astype(F32)

    @pl.when(s_id == 0)
    def _():
        m_s[...] = jnp.full(m_s.shape, neg_inf, F32)
        l_s[...] = jnp.zeros(l_s.shape, F32)
        acc_s[...] = jnp.zeros(acc_s.shape, F32)

    def step(kc, vc, bias, k_is_transposed=False):
        m, l, acc = _softmax_step((m_s[:, 0:1], l_s[:, 0:1], acc_s[...]), q2, kc, vc, bias,
                                  k_is_transposed)
        m_s[...] = jnp.broadcast_to(m, m_s.shape)
        l_s[...] = jnp.broadcast_to(l, l_s.shape)
        acc_s[...] = acc

    for p in range(DEC_PP):
        kpos0 = ((s_id * DEC_PP + p) * PAGE_SIZE).astype(F32)
        dist = (past_len + qi) - (kpos0 + col)
        kt = k_refs[p][0].astype(BF16)
        vc = jnp.concatenate(
            [v_refs[p][0, pl.ds(h, PAGE_SIZE, stride=N_DH), :].astype(BF16) for h in range(N_DH)],
            axis=1)
        step(kt, vc, -srow * dist, k_is_transposed=True)

    @pl.when(s_id == pl.num_programs(1) - 1)
    def _():
        dist = qi - col
        bias = jnp.where(dist >= 0, -srow * dist, neg_inf)
        step(kn_ref[0].astype(BF16), vn_ref[0].astype(BF16), bias)
        lam = _lam_from(lq_ref, lk_ref, lam_init)
        sg = sg_ref[...]
        l = l_s[:, 0:1]
        for h in range(N_DH):
            blk = acc_s[h * 2 * DEC_SEQ:(h + 1) * 2 * DEC_SEQ, h * DV:(h + 1) * DV]
            blk = blk / l[h * 2 * DEC_SEQ:(h + 1) * 2 * DEC_SEQ]
            o = blk[0:DEC_SEQ] - lam * blk[DEC_SEQ:2 * DEC_SEQ]
            o_ref[0, :, h * DV:(h + 1) * DV] = _rms(o, sg) * (1.0 - lam_init)


def _decode_attn(page_table, q_s, k_s, v_s, cache_k, cache_v, slopes_vec, lq, lk, sg, lam_init):
    n_pages = page_table.shape[1]
    n_rows = 2 * N_DH * DEC_SEQ
    q_t = jnp.tile(q_s, (1, 2 * N_DH, 1))
    pad = ((0, 0), (0, PAGE_SIZE - DEC_SEQ), (0, 0))
    kn = jnp.pad(k_s, pad)
    vn = jnp.pad(v_s, pad)
    ridx = jnp.arange(n_rows)
    srow = jnp.broadcast_to(slopes_vec[ridx // (2 * DEC_SEQ)][:, None], (n_rows, PAGE_SIZE)).astype(F32)
    qi = jnp.broadcast_to((ridx % DEC_SEQ)[:, None], (n_rows, PAGE_SIZE)).astype(F32)

    def page_spec(p, shape):
        return pl.BlockSpec((1,) + shape, lambda b, s, pt: (pt[b, s * DEC_PP + p], 0, 0))

    const2 = lambda b, s, pt: (0, 0)
    per_b = lambda b, s, pt: (b, 0, 0)
    grid_spec = pltpu.PrefetchScalarGridSpec(
        num_scalar_prefetch=1,
        grid=(DEC_BATCH, n_pages // DEC_PP),
        in_specs=[pl.BlockSpec((n_rows, PAGE_SIZE), const2),
                  pl.BlockSpec((n_rows, PAGE_SIZE), const2),
                  pl.BlockSpec((2, DK), const2),
                  pl.BlockSpec((2, DK), const2),
                  pl.BlockSpec((1, DV), const2),
                  pl.BlockSpec((1, n_rows, D_ATT), per_b),
                  pl.BlockSpec((1, PAGE_SIZE, D_ATT), per_b),
                  pl.BlockSpec((1, PAGE_SIZE, D_ATT), per_b)]
                 + [page_spec(p, (D_ATT, PAGE_SIZE)) for p in range(DEC_PP)]
                 + [page_spec(p, (PAGE_SIZE * N_DH, DV)) for p in range(DEC_PP)],
        out_specs=pl.BlockSpec((1, DEC_SEQ, D_ATT), per_b),
        scratch_shapes=[pltpu.VMEM((n_rows, LANES), F32), pltpu.VMEM((n_rows, LANES), F32),
                        pltpu.VMEM((n_rows, D_ATT), F32)],
    )
    return pl.pallas_call(
        functools.partial(_decode_attn_body, lam_init=lam_init, n_pages=n_pages),
        grid_spec=grid_spec,
        out_shape=jax.ShapeDtypeStruct((DEC_BATCH, DEC_SEQ, D_ATT), F32),
        compiler_params=_cparams(("parallel", "arbitrary")),
        name="decode_attn",
    )(page_table, srow, qi, lq, lk, sg.reshape(1, DV), q_t, kn, vn,
      *([cache_k] * DEC_PP), *([cache_v] * DEC_PP))


CONV_PRE = 32
CONV_RC = 48
CONV_CT = 256
assert T_P % CONV_RC == 0


def _conv_prompt_body(prev_ref, g_ref, w_ref, b_ref, o_ref, gp):
    gp[0:CONV_PRE, :] = prev_ref[0]
    gp[CONV_PRE:, :] = g_ref[...]
    lead = CONV_PRE - (CONV_W - 1)
    win_rows = CONV_RC + CONV_PRE
    bias = b_ref[...]

    def chunk(ci, _):
        t0 = pl.multiple_of(ci * CONV_RC, SUBLANES)
        win = gp[pl.ds(t0, win_rows), :]
        acc = jnp.broadcast_to(bias, (CONV_RC, CONV_CT))
        for r in range(SUBLANES):
            taps = [k for k in range(CONV_W) if (k + lead) % SUBLANES == r]
            if not taps:
                continue
            hi = max(k + lead for k in taps) - r + CONV_RC
            wr = win[r:r + hi, :]
            for k in taps:
                a = k + lead - r
                acc = acc + w_ref[k:k + 1, :] * wr[a:a + CONV_RC, :]
        o_ref[pl.ds(t0, CONV_RC), :] = acc
        return 0

    lax.fori_loop(0, T_P // CONV_RC, chunk, 0)


def _conv_prompt(g_all, prev, w, b):
    return pl.pallas_call(
        _conv_prompt_body,
        grid=(BATCH, D_CONV // CONV_CT),
        in_specs=[pl.BlockSpec((1, CONV_PRE, CONV_CT), lambda bi, j: (bi, 0, j)),
                  pl.BlockSpec((T_P, CONV_CT), lambda bi, j: (bi, j)),
                  pl.BlockSpec((CONV_W, CONV_CT), lambda bi, j: (0, j)),
                  pl.BlockSpec((1, CONV_CT), lambda bi, j: (0, j))],
        out_specs=pl.BlockSpec((T_P, CONV_CT), lambda bi, j: (bi, j)),
        out_shape=jax.ShapeDtypeStruct((ROWS, D_CONV), F32),
        scratch_shapes=[pltpu.VMEM((CONV_PRE + T_P, CONV_CT), F32)],
        compiler_params=_cparams(("parallel", "parallel")),
        name="conv_prompt",
    )(prev, g_all, w, b.reshape(1, D_CONV))


def _conv_sample_body(ext_ref, w_ref, b_ref, o_ref):
    w = w_ref[...]
    for bi in range(DEC_BATCH):
        for t in range(DEC_SEQ):
            acc = jnp.sum(w * ext_ref[bi, t:t + CONV_W, :], axis=0, keepdims=True) + b_ref[...]
            o_ref[bi * DEC_SEQ + t:bi * DEC_SEQ + t + 1, :] = acc


def _conv_sample(ext, w, b):
    return pl.pallas_call(
        _conv_sample_body,
        out_shape=jax.ShapeDtypeStruct((ROWS_S, D_CONV), F32),
        name="conv_sample",
    )(ext, w, b.reshape(1, D_CONV))


SSM_NK = SSM_L + 1


def _ssm_param_body(lre_ref, lim_ref, ldt_ref, bre_ref, bim_ref, cre_ref, cim_ref,
                    ap_a, ap_b, bp_ref, cp_ref, km_ref,
                    ar_s, ai_s, pr_s, pi_s, bbr_s, bbi_s):
    k = pl.program_id(0)

    @pl.when(k == 0)
    def _():
        lr, li = lre_ref[...], lim_ref[...]
        dt = jnp.exp(ldt_ref[...])
        er = jnp.exp(lr * dt)
        ar = er * jnp.cos(li * dt)
        ai = er * jnp.sin(li * dt)
        den = lr * lr + li * li
        xr, xi = ar - 1.0, ai
        cr = (xr * lr + xi * li) / den
        ci = (xi * lr - xr * li) / den
        br, bi = bre_ref[...], bim_ref[...]
        bbr_s[...] = cr * br - ci * bi
        bbi_s[...] = cr * bi + ci * br
        ar_s[...] = ar
        ai_s[...] = ai
        pr_s[...] = jnp.ones(pr_s.shape, F32)
        pi_s[...] = jnp.zeros(pi_s.shape, F32)

    @pl.when(k > 0)
    def _():
        pr, pi = pr_s[...], pi_s[...]
        ar, ai = ar_s[...], ai_s[...]
        pr_s[...] = pr * ar - pi * ai
        pi_s[...] = pr * ai + pi * ar

    pr, pi = pr_s[...], pi_s[...]
    ap_a[0] = jnp.concatenate([pr, pr], axis=-1)
    ap_b[0] = jnp.concatenate([-pi, pi], axis=-1)
    bbr, bbi = bbr_s[...], bbi_s[...]
    bpr = pr * bbr - pi * bbi
    bpi = pr * bbi + pi * bbr
    bp_ref[0] = jnp.concatenate([bpr, bpi], axis=-1)
    cr, ci = cre_ref[...], cim_ref[...]
    cp_ref[0] = jnp.concatenate([pr * cr - pi * ci, -(pr * ci + pi * cr)], axis=-1)
    dn = (((2,), (2,)), ((0,), (0,)))
    hp = lax.Precision.HIGHEST
    km_ref[0] = (lax.dot_general(bpr, cr, dn, precision=hp, preferred_element_type=F32)
                 - lax.dot_general(bpi, ci, dn, precision=hp, preferred_element_type=F32))


def _ssm_params(lre, lim, log_dt, b_re, b_im, c_re, c_im):
    g, p, c = N_SSM_GROUPS, SSM_P, SSM_GROUP
    bc = lambda a: jnp.broadcast_to(a[:, None, :], (g, c, p))
    ldt = jnp.broadcast_to(log_dt[:, None, None], (g, c, p))
    bt_re = jnp.swapaxes(b_re, 1, 2)
    bt_im = jnp.swapaxes(b_im, 1, 2)
    gcp = pl.BlockSpec((g, c, p), lambda k: (0, 0, 0))
    o_wide = pl.BlockSpec((1, g, c, 2 * p), lambda k: (k, 0, 0, 0))
    wide = jax.ShapeDtypeStruct((SSM_NK, g, c, 2 * p), F32)
    ap_a, ap_b, bp, cp, km = pl.pallas_call(
        _ssm_param_body,
        grid=(SSM_NK,),
        in_specs=[gcp] * 7,
        out_specs=[o_wide] * 4 + [pl.BlockSpec((1, g, c, c), lambda k: (k, 0, 0, 0))],
        out_shape=[wide] * 4 + [jax.ShapeDtypeStruct((SSM_NK, g, c, c), F32)],
        scratch_shapes=[pltpu.VMEM((g, c, p), F32)] * 6,
        compiler_params=_cparams(("arbitrary",)),
        name="ssm_params",
    )(bc(lre), bc(lim), ldt, bt_re, bt_im, c_re, c_im)
    return (ap_a, ap_b, bp.reshape(SSM_NK, g * c, 2 * p), cp.reshape(SSM_NK, g * c, 2 * p),
            km.reshape(SSM_NK, g * c, c))


def _gather_chunks(u_ref, length, m):
    xs = [u_ref[pl.ds(s, m, stride=length), :].astype(BF16) for s in range(length)]
    return jnp.concatenate(xs, axis=1)


SSM_SQ = 2 * SSM_SW // LANES
SSM_SLANES = N_SSM_GROUPS * 2 * SSM_P
NT_DIMS = (((1,), (1,)), ((), ()))


def _ssm_body(u_ref, km_ref, bp_ref, cp_ref, av_ref, bv_ref, h0_ref, d_ref, *rest,
              length, m, n_seq, n_chunks, in_place):
    if in_place:
        rest = rest[1:]
    o_ref, fin_ref, v_s, ws_s, wct_s, s_s, hp_s = rest
    x = _gather_chunks(u_ref, length, m)

    erow = lax.broadcasted_iota(jnp.int32, (SSM_GROUP, LANES), 0)
    elane = lax.broadcasted_iota(jnp.int32, (SSM_GROUP, LANES), 1)
    spread = jnp.where(elane % SSM_GROUP == erow, 1.0, 0.0).astype(BF16)
    rgrp = lax.broadcasted_iota(jnp.int32, (LANES, LANES), 0) // SSM_GROUP
    lgrp = lax.broadcasted_iota(jnp.int32, (LANES, LANES), 1) // SSM_GROUP
    for q in range(length):
        blk = jnp.dot(km_ref[length - 1 - q].astype(BF16), spread, preferred_element_type=F32)
        v_s[q * LANES:(q + 1) * LANES, :] = jnp.where(rgrp == lgrp, blk, 0.0).astype(BF16)
    for s in range(length):
        bblk = bp_ref[length - 1 - s]
        cblk = cp_ref[s + 1]
        for g in range(SSM_G8):
            ws_s[s * LANES:(s + 1) * LANES, g * LANES:(g + 1) * LANES] = (
                jnp.where(rgrp == g, bblk, 0.0).astype(BF16))
            wct_s[s * LANES:(s + 1) * LANES, g * LANES:(g + 1) * LANES] = (
                jnp.where(rgrp == g, cblk, 0.0).astype(BF16))

    s_loc = jnp.dot(x, ws_s[...], preferred_element_type=F32)
    for q in range(SSM_SQ):
        s_s[q] = s_loc[:, q * LANES:(q + 1) * LANES]

    if m != n_seq * n_chunks:
        hp_s[...] = jnp.zeros(hp_s.shape, F32)
    av = [av_ref[:, q * LANES:(q + 1) * LANES] for q in range(SSM_SQ)]
    bv = [bv_ref[:, q * LANES:(q + 1) * LANES] for q in range(SSM_SQ)]

    def advance(rows, hs):
        out = []
        for q in range(SSM_SQ):
            h = hs[q]
            hp_s[q, rows, :] = h
            out.append(av[q] * h + bv[q] * pltpu.roll(h, SSM_P, 1) + s_s[q, rows, :])
        return tuple(out)

    hs = tuple(h0_ref[:, q * LANES:(q + 1) * LANES] for q in range(SSM_SQ))
    if n_chunks == 1:
        hs = advance(pl.ds(0, n_seq), hs)
    else:
        hs = lax.fori_loop(0, n_chunks,
                           lambda c, hs: advance(pl.ds(c, n_seq, stride=n_chunks), hs), hs)
    for q in range(SSM_SQ):
        fin_ref[:, q * LANES:(q + 1) * LANES] = hs[q]

    hp = jnp.concatenate([hp_s[q].astype(BF16) for q in range(SSM_SQ)], axis=1)
    d = d_ref[...]
    for t in range(length):
        rows = pl.ds(t, m, stride=length)
        y = jnp.dot(x[:, :(t + 1) * LANES], v_s[(length - 1 - t) * LANES:, :],
                    preferred_element_type=F32)
        y = y + lax.dot_general(hp, wct_s[t * LANES:(t + 1) * LANES, :], NT_DIMS,
                                preferred_element_type=F32)
        o_ref[rows, :] = y + d * u_ref[rows, :]


def _ssm_apply(u_all, y_all, km, bp, cp, av, bv, h0, dvec, *, length, m, n_seq, n_chunks, row_blk):
    in_place = y_all is not None
    rows = pl.BlockSpec((m * length, LANES), lambda j: (row_blk, j))
    comp = lambda last: pl.BlockSpec((SSM_NK, LANES, last), lambda j: (0, j, 0))
    vec = pl.BlockSpec((1, SSM_SQ * LANES), lambda j: (0, j))
    st = pl.BlockSpec((n_seq, SSM_SQ * LANES), lambda j: (0, j))
    in_specs = [rows, comp(SSM_GROUP), comp(LANES), comp(LANES), vec, vec, st,
                pl.BlockSpec((1, LANES), lambda j: (0, j))]
    args = [u_all, km, bp, cp, av, bv, h0, dvec]
    if in_place:
        in_specs.append(pl.BlockSpec(memory_space=pl.ANY))
        args.append(y_all)
    return pl.pallas_call(
        functools.partial(_ssm_body, length=length, m=m, n_seq=n_seq, n_chunks=n_chunks,
                          in_place=in_place),
        grid=(SSM_NJ,),
        in_specs=in_specs,
        out_specs=[rows, st],
        out_shape=[jax.ShapeDtypeStruct((ROWS, D_MODEL), F32),
                   jax.ShapeDtypeStruct((n_seq, SSM_SLANES), F32)],
        input_output_aliases={len(args) - 1: 0} if in_place else {},
        scratch_shapes=[pltpu.VMEM((length * LANES, LANES), BF16),
                        pltpu.VMEM((length * LANES, SSM_SQ * LANES), BF16),
                        pltpu.VMEM((length * LANES, SSM_SQ * LANES), BF16),
                        pltpu.VMEM((SSM_SQ, m, LANES), F32),
                        pltpu.VMEM((SSM_SQ, m, LANES), F32)],
        compiler_params=_cparams(("parallel",)),
        name="ssm_sample" if in_place else "ssm_prompt",
    )(*args)


def kernel(x_prompt, x_sample, cache_k, cache_v, state_conv, state_ssm_re, state_ssm_im, page_table, meta_tokens, norm_mix_pre, norm_mix_post, norm_ffn_pre, norm_ffn_post, w_in_even, lambda_q, lambda_k, subln_g, conv_w, conv_b, conv_ln_g, conv_ln_b, w_out_even, w_in_odd, ssm_a_re, ssm_a_im, ssm_b_re, ssm_b_im, ssm_c_re, ssm_c_im, ssm_d, ssm_log_dt, w_glu, w_out_odd, w_ffn_gate, w_ffn_up, w_ffn_down):
    n_pool = cache_k.shape[1]
    meta = jnp.broadcast_to(meta_tokens.astype(F32)[None], (BATCH, N_META, D_MODEL))
    x = jnp.concatenate([
        jnp.concatenate([meta, x_prompt], axis=1).reshape(ROWS_P, D_MODEL),
        x_sample.reshape(ROWS_S, D_MODEL),
        jnp.zeros((ROWS - ROWS_P - ROWS_S, D_MODEL), F32)], axis=0)
    slopes_vec = jnp.array([2.0 ** (-8.0 * (i + 1) / N_DH) for i in range(N_DH)], dtype=F32)
    slopes = jnp.broadcast_to(slopes_vec[:, None, None], (N_DH, 1, LANES))
    sample = slice(ROWS_P, ROWS_P + ROWS_S)

    ks, vs, convs_p, convs_s, sre_p, sim_p, sre_s, sim_s = [], [], [], [], [], [], [], []
    for layer in range(DEPTH):
        if layer % 2 == 0:
            e = layer // 2
            lam_init = 0.8 - 0.6 * math.exp(-0.3 * (2 * e))
            w_in = w_in_even[e].astype(BF16)
            gpre = norm_mix_pre[layer]
            q_all = _norm_mm(x, gpre, [w_in[:, 0:D_ATT]])
            k_all = _norm_mm(x, gpre, [w_in[:, D_ATT:2 * D_ATT]])
            v_all = _norm_mm(x, gpre, [w_in[:, 2 * D_ATT:3 * D_ATT]])
            g_all = _norm_mm(x, gpre, [w_in[:, 3 * D_ATT:3 * D_ATT + D_CONV],
                                       w_in[:, 3 * D_ATT + D_CONV:]])
            o_all = _prompt_attn(q_all, k_all, v_all, slopes, lambda_q[e], lambda_k[e], subln_g[e],
                                 lam_init)
            o_s = _decode_attn(page_table,
                               q_all[sample].reshape(DEC_BATCH, DEC_SEQ, D_ATT),
                               k_all[sample].reshape(DEC_BATCH, DEC_SEQ, D_ATT),
                               v_all[sample].reshape(DEC_BATCH, DEC_SEQ, D_ATT),
                               jnp.transpose(cache_k[e], (0, 2, 3, 4, 1)).reshape(n_pool, D_ATT, PAGE_SIZE),
                               cache_v[e].reshape(n_pool, PAGE_SIZE * N_DH, DV),
                               slopes_vec, lambda_q[e], lambda_k[e], subln_g[e], lam_init)
            tail = jnp.zeros((ROWS - ROWS_P - ROWS_S, D_ATT), F32)
            o_all = lax.dynamic_update_slice(
                o_all, jnp.concatenate([o_s.reshape(ROWS_S, D_ATT), tail], axis=0), (ROWS_P, 0))
            g_s = g_all[sample].reshape(DEC_BATCH, DEC_SEQ, D_CONV)
            ext_s = jnp.concatenate([state_conv[e], g_s], axis=1)
            c_all = _conv_prompt(g_all, jnp.zeros((BATCH, CONV_PRE, D_CONV), F32), conv_w[e], conv_b[e])
            c_s = _conv_sample(ext_s, conv_w[e], conv_b[e])
            c_all = lax.dynamic_update_slice(c_all, jnp.concatenate([c_s, tail], axis=0), (ROWS_P, 0))
            w_out = w_out_even[e].astype(BF16)
            x = _even_out(o_all, c_all, conv_ln_g[e], conv_ln_b[e], w_out[:D_ATT], w_out[D_ATT:],
                          norm_mix_post[layer], x)
            ks.append((k_all[:ROWS_P].reshape(BATCH, T_P, N_DH, 2, DK),
                       k_all[sample].reshape(DEC_BATCH, DEC_SEQ, N_DH, 2, DK)))
            vs.append((v_all[:ROWS_P].reshape(BATCH, T_P, N_DH, DV),
                       v_all[sample].reshape(DEC_BATCH, DEC_SEQ, N_DH, DV)))
            convs_p.append(g_all[:ROWS_P].reshape(BATCH, T_P, D_CONV)[:, T_P - (CONV_W - 1):])
            convs_s.append(ext_s[:, DEC_SEQ:])
        else:
            o = layer // 2
            u_all = _norm_mm(x, norm_mix_pre[layer], [w_in_odd[o].astype(BF16)])
            ap_a, ap_b, bp, cp, km = _ssm_params(
                ssm_a_re[o], ssm_a_im[o], ssm_log_dt[o], ssm_b_re[o], ssm_b_im[o],
                ssm_c_re[o], ssm_c_im[o])
            dvec = ssm_d[o].reshape(1, D_MODEL)
            step_vec = lambda a, k: a[k, :, 0, :].reshape(1, SSM_SLANES)
            y_all, f_p = _ssm_apply(
                u_all, None, km, bp, cp, step_vec(ap_a, SSM_L), step_vec(ap_b, SSM_L),
                jnp.zeros((BATCH, SSM_SLANES), F32), dvec,
                length=SSM_L, m=SSM_M, n_seq=BATCH, n_chunks=SSM_NC, row_blk=0)
            h0 = jnp.concatenate([state_ssm_re[o], state_ssm_im[o]], axis=-1)
            y_all, f_s = _ssm_apply(
                u_all, y_all, km, bp, cp, step_vec(ap_a, DEC_SEQ), step_vec(ap_b, DEC_SEQ),
                h0.reshape(DEC_BATCH, SSM_SLANES), dvec,
                length=DEC_SEQ, m=DEC_BATCH, n_seq=DEC_BATCH, n_chunks=1, row_blk=SAMPLE_BLK)
            x = _odd_out(y_all, x, w_glu[o].astype(BF16), w_out_odd[o].astype(BF16),
                         norm_mix_post[layer])
            f_p = f_p.reshape(BATCH, N_SSM_GROUPS, 2, SSM_P)
            f_s = f_s.reshape(DEC_BATCH, N_SSM_GROUPS, 2, SSM_P)
            sre_p.append(f_p[:, :, 0])
            sim_p.append(f_p[:, :, 1])
            sre_s.append(f_s[:, :, 0])
            sim_s.append(f_s[:, :, 1])
        x = _ffn(x, norm_ffn_pre[layer], w_ffn_gate[layer].astype(BF16), w_ffn_up[layer].astype(BF16),
                 w_ffn_down[layer].astype(BF16), norm_ffn_post[layer])

    y_prompt = x[:ROWS_P].reshape(BATCH, T_P, D_MODEL)[:, N_META:]
    y_sample = x[sample].reshape(DEC_BATCH, DEC_SEQ, D_MODEL)
    return (y_prompt, y_sample,
            jnp.stack([k[0] for k in ks]), jnp.stack([v[0] for v in vs]),
            jnp.stack([k[1] for k in ks]), jnp.stack([v[1] for v in vs]),
            jnp.stack(convs_p), jnp.stack(convs_s),
            jnp.stack(sre_p), jnp.stack(sim_p), jnp.stack(sre_s), jnp.stack(sim_s))
```

```python
import functools
import math

import jax
import jax.numpy as jnp
from jax import lax
from jax.experimental import pallas as pl
from jax.experimental.pallas import tpu as pltpu

F32 = jnp.float32
BF16 = jnp.bfloat16

D_MODEL = 2048
BATCH = 4
SEQ = 2048
DEPTH = 2
DEC_BATCH = 8
DEC_SEQ = 4
PAGE_SIZE = 128
N_META = 16
D_ATT = D_MODEL // 2
N_DH = 8
DK = D_ATT // N_DH // 2
DV = 2 * DK
D_CONV = D_MODEL - D_ATT
CONV_W = 31
SSM_GROUP = 16
N_SSM_GROUPS = D_MODEL // SSM_GROUP
SSM_P = 64
D_FF = ((8 * D_MODEL // 3 + 255) // 256) * 256
EPS = 1e-6

T_P = N_META + SEQ
ROWS_P = BATCH * T_P
ROWS_S = DEC_BATCH * DEC_SEQ
TM = 640
ROWS = 8320
assert ROWS % TM == 0 and ROWS >= ROWS_P + ROWS_S
assert ROWS_P % ROWS_S == 0
SAMPLE_BLK = ROWS_P // ROWS_S

LANES = 128
SUBLANES = 8
VMEM_LIMIT = 56 * 1024 * 1024

SSM_L = 16
SSM_NC = T_P // SSM_L
SSM_M = ROWS // SSM_L
SSM_NJ = D_MODEL // LANES
SSM_G8 = LANES // SSM_GROUP
SSM_SW = SSM_G8 * SSM_P
assert T_P % SSM_L == 0 and ROWS % SSM_L == 0

ATT_T = 256
ATT_NT = SEQ // ATT_T
NT_DIMS = (((1,), (1,)), ((), ()))
DEC_PP = 8


def _cparams(sem, vmem=VMEM_LIMIT):
    return pltpu.CompilerParams(dimension_semantics=sem, vmem_limit_bytes=vmem)


def _rms(x, g):
    ms = jnp.mean(x * x, axis=-1, keepdims=True)
    return x * lax.rsqrt(ms + EPS) * g


def _norm_mm_body(x_ref, g_ref, *refs, glu):
    n_w = 2 if glu else 1
    w_refs, o_ref, hb = refs[:n_w], refs[n_w], refs[n_w + 1]

    @pl.when(pl.program_id(1) == 0)
    def _():
        hb[...] = _rms(x_ref[...], g_ref[...]).astype(BF16)

    h = hb[...]
    a = jnp.dot(h, w_refs[0][...], preferred_element_type=F32)
    if glu:
        b = jnp.dot(h, w_refs[1][...], preferred_element_type=F32)
        a = a * jax.nn.sigmoid(b)
    o_ref[...] = a


def _norm_mm(x, g, ws, *, tn=512):
    rows, d = x.shape
    n = ws[0].shape[1]
    glu = len(ws) == 2
    return pl.pallas_call(
        functools.partial(_norm_mm_body, glu=glu),
        grid=(rows // TM, n // tn),
        in_specs=[pl.BlockSpec((TM, d), lambda i, j: (i, 0)),
                  pl.BlockSpec((1, d), lambda i, j: (0, 0))]
                 + [pl.BlockSpec((d, tn), lambda i, j: (0, j)) for _ in ws],
        out_specs=pl.BlockSpec((TM, tn), lambda i, j: (i, j)),
        out_shape=jax.ShapeDtypeStruct((rows, n), F32),
        scratch_shapes=[pltpu.VMEM((TM, d), BF16)],
        compiler_params=_cparams(("parallel", "arbitrary")),
        name="norm_mm_glu" if glu else "norm_mm",
    )(x, g.reshape(1, d), *ws)


def _ffn_body(x_ref, gpre_ref, wg_ref, wu_ref, wd_ref, gpost_ref, o_ref, hb, acc):
    j = pl.program_id(1)

    @pl.when(j == 0)
    def _():
        hb[...] = _rms(x_ref[...], gpre_ref[...]).astype(BF16)

    h = hb[...]
    gate = jnp.dot(h, wg_ref[...], preferred_element_type=F32)
    up = jnp.dot(h, wu_ref[...], preferred_element_type=F32)
    a = (jax.nn.silu(gate) * up).astype(BF16)
    part = jnp.dot(a, wd_ref[...], preferred_element_type=F32)

    @pl.when(j == 0)
    def _():
        acc[...] = part

    @pl.when(j > 0)
    def _():
        acc[...] += part

    @pl.when(j == pl.num_programs(1) - 1)
    def _():
        o_ref[...] = x_ref[...] + _rms(acc[...], gpost_ref[...])


def _ffn(x, gpre, wg, wu, wd, gpost, *, tf=512):
    rows, d = x.shape
    dff = wg.shape[1]
    return pl.pallas_call(
        _ffn_body,
        grid=(rows // TM, dff // tf),
        in_specs=[pl.BlockSpec((TM, d), lambda i, j: (i, 0)),
                  pl.BlockSpec((1, d), lambda i, j: (0, 0)),
                  pl.BlockSpec((d, tf), lambda i, j: (0, j)),
                  pl.BlockSpec((d, tf), lambda i, j: (0, j)),
                  pl.BlockSpec((tf, d), lambda i, j: (j, 0)),
                  pl.BlockSpec((1, d), lambda i, j: (0, 0))],
        out_specs=pl.BlockSpec((TM, d), lambda i, j: (i, 0)),
        out_shape=jax.ShapeDtypeStruct((rows, d), F32),
        scratch_shapes=[pltpu.VMEM((TM, d), BF16), pltpu.VMEM((TM, d), F32)],
        compiler_params=_cparams(("parallel", "arbitrary")),
        name="ffn",
    )(x, gpre.reshape(1, d), wg, wu, wd, gpost.reshape(1, d))


def _odd_out_body(y_ref, res_ref, wglu_ref, wout_ref, gpost_ref, o_ref, yf, hb, acc, *, tf):
    j = pl.program_id(1)
    nj = pl.num_programs(1)

    @pl.when(j == 0)
    def _():
        for jj in range(yf.shape[0]):
            gy = jax.nn.gelu(y_ref[:, jj * tf:(jj + 1) * tf])
            yf[jj] = gy
            hb[:, jj * tf:(jj + 1) * tf] = gy.astype(BF16)

    t = jnp.dot(hb[...], wglu_ref[...], preferred_element_type=F32)
    a = (yf[j] * jax.nn.sigmoid(t)).astype(BF16)
    part = jnp.dot(a, wout_ref[...], preferred_element_type=F32)

    @pl.when(j == 0)
    def _():
        acc[...] = part

    @pl.when(j > 0)
    def _():
        acc[...] += part

    @pl.when(j == nj - 1)
    def _():
        o_ref[...] = res_ref[...] + _rms(acc[...], gpost_ref[...])


def _odd_out(y, res, wglu, wout, gpost, *, tf=512, tm=TM // 2):
    rows, d = y.shape
    return pl.pallas_call(
        functools.partial(_odd_out_body, tf=tf),
        grid=(rows // tm, d // tf),
        in_specs=[pl.BlockSpec((tm, d), lambda i, j: (i, 0)),
                  pl.BlockSpec((tm, d), lambda i, j: (i, 0)),
                  pl.BlockSpec((d, tf), lambda i, j: (0, j)),
                  pl.BlockSpec((tf, d), lambda i, j: (j, 0)),
                  pl.BlockSpec((1, d), lambda i, j: (0, 0))],
        out_specs=pl.BlockSpec((tm, d), lambda i, j: (i, 0)),
        out_shape=jax.ShapeDtypeStruct((rows, d), F32),
        scratch_shapes=[pltpu.VMEM((d // tf, tm, tf), F32), pltpu.VMEM((tm, d), BF16),
                        pltpu.VMEM((tm, d), F32)],
        compiler_params=_cparams(("parallel", "arbitrary")),
        name="odd_out",
    )(y, res, wglu, wout, gpost.reshape(1, d))


def _even_out_body(o_ref, c_ref, lng_ref, lnb_ref, wtop_ref, wbot_ref, gpost_ref, res_ref, out_ref):
    c = c_ref[...]
    mu = jnp.mean(c, axis=-1, keepdims=True)
    var = jnp.mean(jnp.square(c - mu), axis=-1, keepdims=True)
    cn = jax.nn.silu((c - mu) * lax.rsqrt(var + EPS) * lng_ref[...] + lnb_ref[...])
    y = jnp.dot(o_ref[...].astype(BF16), wtop_ref[...], preferred_element_type=F32)
    y = y + jnp.dot(cn.astype(BF16), wbot_ref[...], preferred_element_type=F32)
    out_ref[...] = res_ref[...] + _rms(y, gpost_ref[...])


def _even_out(o, c, lng, lnb, wtop, wbot, gpost, res):
    rows, d = res.shape
    da, dc = o.shape[1], c.shape[1]
    return pl.pallas_call(
        _even_out_body,
        grid=(rows // TM,),
        in_specs=[pl.BlockSpec((TM, da), lambda i: (i, 0)),
                  pl.BlockSpec((TM, dc), lambda i: (i, 0)),
                  pl.BlockSpec((1, dc), lambda i: (0, 0)),
                  pl.BlockSpec((1, dc), lambda i: (0, 0)),
                  pl.BlockSpec((da, d), lambda i: (0, 0)),
                  pl.BlockSpec((dc, d), lambda i: (0, 0)),
                  pl.BlockSpec((1, d), lambda i: (0, 0)),
                  pl.BlockSpec((TM, d), lambda i: (i, 0))],
        out_specs=pl.BlockSpec((TM, d), lambda i: (i, 0)),
        out_shape=jax.ShapeDtypeStruct((rows, d), F32),
        compiler_params=_cparams(("parallel",)),
        name="even_out",
    )(o, c, lng.reshape(1, dc), lnb.reshape(1, dc), wtop, wbot, gpost.reshape(1, d), res)


def _lam_from(lq_ref, lk_ref, lam_init):
    s0 = jnp.sum(lq_ref[0:1, :] * lk_ref[0:1, :], axis=-1, keepdims=True)
    s1 = jnp.sum(lq_ref[1:2, :] * lk_ref[1:2, :], axis=-1, keepdims=True)
    return jnp.exp(s0) - jnp.exp(s1) + lam_init


def _softmax_step(carry, q2, kc, vc, bias, k_is_transposed=False):
    m, l, acc = carry
    if k_is_transposed:
        s = jnp.dot(q2, kc, preferred_element_type=F32) + bias
    else:
        s = lax.dot_general(q2, kc, (((1,), (1,)), ((), ())), preferred_element_type=F32) + bias
    m_new = jnp.maximum(m, jnp.max(s, axis=-1, keepdims=True))
    alpha = jnp.exp(m - m_new)
    p = jnp.exp(s - m_new)
    l = alpha * l + jnp.sum(p, axis=-1, keepdims=True)
    acc = alpha * acc + jnp.dot(p.astype(BF16), vc, preferred_element_type=F32)
    return m_new, l, acc


def _stack_maps(q, scale):
    lane = lax.broadcasted_iota(jnp.int32, q.shape, 1)
    qs = q * scale
    q0 = jnp.where(lane < DK, qs, 0.0)
    q1 = jnp.where(lane >= DK, qs, 0.0)
    return jnp.concatenate([q0, q1], axis=0).astype(BF16)


def _diff_out(m, l, acc, n, lam, sg, lam_init):
    o = acc[:n] / l[:n] - lam * (acc[n:] / l[n:])
    return _rms(o, sg) * (1.0 - lam_init)


def _prompt_attn_body(slope_ref, lq_ref, lk_ref, sg_ref, q_ref, k_ref, v_ref, o_ref,
                      kb, vb, *, lam_init):
    slope = slope_ref[0, 0:1, 0:1]
    lam = _lam_from(lq_ref, lk_ref, lam_init)
    sg = sg_ref[...]
    scale = DK ** -0.5
    kb[...] = k_ref[...].astype(BF16)
    vb[...] = v_ref[...].astype(BF16)
    neg_inf = float("-inf")

    zpad = jnp.zeros((LANES - N_META, LANES), BF16)
    k_meta = jnp.concatenate([kb[0:N_META, :], zpad], axis=0)
    v_meta = jnp.concatenate([vb[0:N_META, :], zpad], axis=0)

    rm = lax.broadcasted_iota(jnp.int32, (2 * N_META, LANES), 0) % N_META
    cm = lax.broadcasted_iota(jnp.int32, (2 * N_META, LANES), 1)
    bias_m = jnp.where(cm <= rm, -slope * (rm - cm).astype(F32), neg_inf)
    q2 = _stack_maps(q_ref[0:N_META, :], scale)
    init = (jnp.full((2 * N_META, 1), neg_inf, F32), jnp.zeros((2 * N_META, 1), F32),
            jnp.zeros((2 * N_META, LANES), F32))
    m, l, acc = _softmax_step(init, q2, k_meta, v_meta, bias_m)
    o_ref[0:N_META, :] = _diff_out(m, l, acc, N_META, lam, sg, lam_init)

    r = lax.broadcasted_iota(jnp.int32, (2 * ATT_T, ATT_T), 0) % ATT_T
    c = lax.broadcasted_iota(jnp.int32, (2 * ATT_T, ATT_T), 1)
    base = slope * (r - c).astype(F32)
    bias_diag = jnp.where(c <= r, -base, neg_inf)
    rmeta = lax.broadcasted_iota(jnp.int32, (2 * ATT_T, LANES), 0) % ATT_T
    cmeta = lax.broadcasted_iota(jnp.int32, (2 * ATT_T, LANES), 1)
    base_meta = slope * (rmeta - cmeta + N_META).astype(F32)
    meta_ok = cmeta < N_META

    def q_tile(i, _):
        q0 = pl.multiple_of(N_META + i * ATT_T, 16)
        q2 = _stack_maps(q_ref[pl.ds(q0, ATT_T), :], scale)
        off = jnp.asarray(i * ATT_T, F32)
        bias_meta = jnp.where(meta_ok, -(base_meta + slope * off), neg_inf)
        carry = (jnp.full((2 * ATT_T, 1), neg_inf, F32), jnp.zeros((2 * ATT_T, 1), F32),
                 jnp.zeros((2 * ATT_T, LANES), F32))
        carry = _softmax_step(carry, q2, k_meta, v_meta, bias_meta)

        def k_tile(j, carry):
            k0 = pl.multiple_of(N_META + j * ATT_T, 16)
            d = jnp.asarray((i - j) * ATT_T, F32)
            return _softmax_step(carry, q2, kb[pl.ds(k0, ATT_T), :], vb[pl.ds(k0, ATT_T), :],
                                 -(base + slope * d))

        carry = lax.fori_loop(0, i, k_tile, carry)
        m, l, acc = _softmax_step(carry, q2, kb[pl.ds(q0, ATT_T), :], vb[pl.ds(q0, ATT_T), :],
                                  bias_diag)
        o_ref[pl.ds(q0, ATT_T), :] = _diff_out(m, l, acc, ATT_T, lam, sg, lam_init)
        return 0

    lax.fori_loop(0, ATT_NT, q_tile, 0)


def _prompt_attn(q_all, k_all, v_all, slopes, lq, lk, sg, lam_init):
    blk = pl.BlockSpec((T_P, DV), lambda b, h: (b, h))
    return pl.pallas_call(
        functools.partial(_prompt_attn_body, lam_init=lam_init),
        grid=(BATCH, N_DH),
        in_specs=[pl.BlockSpec((1, 1, LANES), lambda b, h: (h, 0, 0)),
                  pl.BlockSpec((2, DK), lambda b, h: (0, 0)),
                  pl.BlockSpec((2, DK), lambda b, h: (0, 0)),
                  pl.BlockSpec((1, DV), lambda b, h: (0, 0)),
                  blk, blk, blk],
        out_specs=blk,
        out_shape=jax.ShapeDtypeStruct((ROWS, D_ATT), F32),
        scratch_shapes=[pltpu.VMEM((T_P, DV), BF16), pltpu.VMEM((T_P, DV), BF16)],
        compiler_params=_cparams(("parallel", "parallel")),
        name="prompt_attn",
    )(slopes, lq, lk, sg.reshape(1, DV), q_all, k_all, v_all)


def _decode_attn_body(pt_ref, srow_ref, qi_ref, lq_ref, lk_ref, sg_ref, q_ref, kn_ref, vn_ref, *refs,
                      lam_init, n_pages):
    k_refs, v_refs = refs[:DEC_PP], refs[DEC_PP:2 * DEC_PP]
    o_ref, m_s, l_s, acc_s = refs[2 * DEC_PP:]
    s_id = pl.program_id(1)
    n_rows = 2 * N_DH * DEC_SEQ
    past_len = n_pages * PAGE_SIZE
    neg_inf = float("-inf")

    row = lax.broadcasted_iota(jnp.int32, (n_rows, D_ATT), 0)
    lane = lax.broadcasted_iota(jnp.int32, (n_rows, D_ATT), 1)
    q2 = jnp.where(lane // DK == row // DEC_SEQ, q_ref[0] * (DK ** -0.5), 0.0).astype(BF16)
    srow = srow_ref[...]
    qi = qi_ref[...]
    col = lax.broadcasted_iota(jnp.int32, (n_rows, PAGE_SIZE), 1).astype(F32)
    srow_w = jnp.concatenate([srow] * DEC_PP, axis=1)
    qi_w = jnp.concatenate([qi] * DEC_PP, axis=1)
    col_w = lax.broadcasted_iota(jnp.int32, (n_rows, DEC_PP * PAGE_SIZE), 1).astype(F32)

    @pl.when(s_id == 0)
    def _():
        m_s[...] = jnp.full(m_s.shape, neg_inf, F32)
        l_s[...] = jnp.zeros(l_s.shape, F32)
        acc_s[...] = jnp.zeros(acc_s.shape, F32)

    def step(kc, vc, bias, k_is_transposed=False):
        m, l, acc = _softmax_step((m_s[:, 0:1], l_s[:, 0:1], acc_s[...]), q2, kc, vc, bias,
                                  k_is_transposed)
        m_s[...] = jnp.broadcast_to(m, m_s.shape)
        l_s[...] = jnp.broadcast_to(l, l_s.shape)
        acc_s[...] = acc

    kt = jnp.concatenate([k_refs[p][0].astype(BF16) for p in range(DEC_PP)], axis=1)
    vc = jnp.concatenate(
        [jnp.concatenate(
            [v_refs[p][0, pl.ds(h, PAGE_SIZE, stride=N_DH), :].astype(BF16) for h in range(N_DH)],
            axis=1) for p in range(DEC_PP)], axis=0)
    kpos0 = jnp.asarray(s_id * (DEC_PP * PAGE_SIZE), F32)
    dist = (past_len + qi_w) - (kpos0 + col_w)
    step(kt, vc, -srow_w * dist, k_is_transposed=True)

    @pl.when(s_id == pl.num_programs(1) - 1)
    def _():
        dist = qi - col
        bias = jnp.where(dist >= 0, -srow * dist, neg_inf)
        step(kn_ref[0].astype(BF16), vn_ref[0].astype(BF16), bias)
        lam = _lam_from(lq_ref, lk_ref, lam_init)
        sg = sg_ref[...]
        l = l_s[:, 0:1]
        for h in range(N_DH):
            blk = acc_s[h * 2 * DEC_SEQ:(h + 1) * 2 * DEC_SEQ, h * DV:(h + 1) * DV]
            blk = blk / l[h * 2 * DEC_SEQ:(h + 1) * 2 * DEC_SEQ]
            o = blk[0:DEC_SEQ] - lam * blk[DEC_SEQ:2 * DEC_SEQ]
            o_ref[0, :, h * DV:(h + 1) * DV] = _rms(o, sg) * (1.0 - lam_init)


def _decode_attn(page_table, q_s, k_s, v_s, cache_k, cache_v, slopes_vec, lq, lk, sg, lam_init):
    n_pages = page_table.shape[1]
    n_rows = 2 * N_DH * DEC_SEQ
    q_t = jnp.tile(q_s, (1, 2 * N_DH, 1))
    pad = ((0, 0), (0, PAGE_SIZE - DEC_SEQ), (0, 0))
    kn = jnp.pad(k_s, pad)
    vn = jnp.pad(v_s, pad)
    ridx = jnp.arange(n_rows)
    srow = jnp.broadcast_to(slopes_vec[ridx // (2 * DEC_SEQ)][:, None], (n_rows, PAGE_SIZE)).astype(F32)
    qi = jnp.broadcast_to((ridx % DEC_SEQ)[:, None], (n_rows, PAGE_SIZE)).astype(F32)

    def page_spec(p, shape):
        return pl.BlockSpec((1,) + shape, lambda b, s, pt: (pt[b, s * DEC_PP + p], 0, 0))

    const2 = lambda b, s, pt: (0, 0)
    per_b = lambda b, s, pt: (b, 0, 0)
    grid_spec = pltpu.PrefetchScalarGridSpec(
        num_scalar_prefetch=1,
        grid=(DEC_BATCH, n_pages // DEC_PP),
        in_specs=[pl.BlockSpec((n_rows, PAGE_SIZE), const2),
                  pl.BlockSpec((n_rows, PAGE_SIZE), const2),
                  pl.BlockSpec((2, DK), const2),
                  pl.BlockSpec((2, DK), const2),
                  pl.BlockSpec((1, DV), const2),
                  pl.BlockSpec((1, n_rows, D_ATT), per_b),
                  pl.BlockSpec((1, PAGE_SIZE, D_ATT), per_b),
                  pl.BlockSpec((1, PAGE_SIZE, D_ATT), per_b)]
                 + [page_spec(p, (D_ATT, PAGE_SIZE)) for p in range(DEC_PP)]
                 + [page_spec(p, (PAGE_SIZE * N_DH, DV)) for p in range(DEC_PP)],
        out_specs=pl.BlockSpec((1, DEC_SEQ, D_ATT), per_b),
        scratch_shapes=[pltpu.VMEM((n_rows, LANES), F32), pltpu.VMEM((n_rows, LANES), F32),
                        pltpu.VMEM((n_rows, D_ATT), F32)],
    )
    return pl.pallas_call(
        functools.partial(_decode_attn_body, lam_init=lam_init, n_pages=n_pages),
        grid_spec=grid_spec,
        out_shape=jax.ShapeDtypeStruct((DEC_BATCH, DEC_SEQ, D_ATT), F32),
        compiler_params=_cparams(("parallel", "arbitrary")),
        name="decode_attn",
    )(page_table, srow, qi, lq, lk, sg.reshape(1, DV), q_t, kn, vn,
      *([cache_k] * DEC_PP), *([cache_v] * DEC_PP))


CONV_PRE = 32
CONV_RC = 48
CONV_CT = 256
assert T_P % CONV_RC == 0


def _conv_prompt_body(prev_ref, g_ref, w_ref, b_ref, o_ref, gp):
    gp[0:CONV_PRE, :] = prev_ref[0]
    gp[CONV_PRE:, :] = g_ref[...]
    lead = CONV_PRE - (CONV_W - 1)
    win_rows = CONV_RC + CONV_PRE
    bias = b_ref[...]

    def chunk(ci, _):
        t0 = pl.multiple_of(ci * CONV_RC, SUBLANES)
        win = gp[pl.ds(t0, win_rows), :]
        acc = jnp.broadcast_to(bias, (CONV_RC, CONV_CT))
        for r in range(SUBLANES):
            taps = [k for k in range(CONV_W) if (k + lead) % SUBLANES == r]
            if not taps:
                continue
            hi = max(k + lead for k in taps) - r + CONV_RC
            wr = win[r:r + hi, :]
            for k in taps:
                a = k + lead - r
                acc = acc + w_ref[k:k + 1, :] * wr[a:a + CONV_RC, :]
        o_ref[pl.ds(t0, CONV_RC), :] = acc
        return 0

    lax.fori_loop(0, T_P // CONV_RC, chunk, 0)


def _conv_prompt(g_all, prev, w, b):
    return pl.pallas_call(
        _conv_prompt_body,
        grid=(BATCH, D_CONV // CONV_CT),
        in_specs=[pl.BlockSpec((1, CONV_PRE, CONV_CT), lambda bi, j: (bi, 0, j)),
                  pl.BlockSpec((T_P, CONV_CT), lambda bi, j: (bi, j)),
                  pl.BlockSpec((CONV_W, CONV_CT), lambda bi, j: (0, j)),
                  pl.BlockSpec((1, CONV_CT), lambda bi, j: (0, j))],
        out_specs=pl.BlockSpec((T_P, CONV_CT), lambda bi, j: (bi, j)),
        out_shape=jax.ShapeDtypeStruct((ROWS, D_CONV), F32),
        scratch_shapes=[pltpu.VMEM((CONV_PRE + T_P, CONV_CT), F32)],
        compiler_params=_cparams(("parallel", "parallel")),
        name="conv_prompt",
    )(prev, g_all, w, b.reshape(1, D_CONV))


def _conv_sample_body(ext_ref, w_ref, b_ref, o_ref):
    w = w_ref[...]
    for bi in range(DEC_BATCH):
        for t in range(DEC_SEQ):
            acc = jnp.sum(w * ext_ref[bi, t:t + CONV_W, :], axis=0, keepdims=True) + b_ref[...]
            o_ref[bi * DEC_SEQ + t:bi * DEC_SEQ + t + 1, :] = acc


def _conv_sample(ext, w, b):
    return pl.pallas_call(
        _conv_sample_body,
        out_shape=jax.ShapeDtypeStruct((ROWS_S, D_CONV), F32),
        name="conv_sample",
    )(ext, w, b.reshape(1, D_CONV))


SSM_NK = SSM_L + 1


def _ssm_param_body(lre_ref, lim_ref, ldt_ref, bre_ref, bim_ref, cre_ref, cim_ref,
                    ap_re, ap_im, bp_rr, bp_ii, cp_rr, cp_ii, km_ref,
                    ar_s, ai_s, pr_s, pi_s, bbr_s, bbi_s):
    k = pl.program_id(0)

    @pl.when(k == 0)
    def _():
        lr, li = lre_ref[...], lim_ref[...]
        dt = jnp.exp(ldt_ref[...])
        er = jnp.exp(lr * dt)
        ar = er * jnp.cos(li * dt)
        ai = er * jnp.sin(li * dt)
        den = lr * lr + li * li
        xr, xi = ar - 1.0, ai
        cr = (xr * lr + xi * li) / den
        ci = (xi * lr - xr * li) / den
        br, bi = bre_ref[...], bim_ref[...]
        bbr_s[...] = cr * br - ci * bi
        bbi_s[...] = cr * bi + ci * br
        ar_s[...] = ar
        ai_s[...] = ai
        pr_s[...] = jnp.ones(pr_s.shape, F32)
        pi_s[...] = jnp.zeros(pi_s.shape, F32)

    @pl.when(k > 0)
    def _():
        pr, pi = pr_s[...], pi_s[...]
        ar, ai = ar_s[...], ai_s[...]
        pr_s[...] = pr * ar - pi * ai
        pi_s[...] = pr * ai + pi * ar

    pr, pi = pr_s[...], pi_s[...]
    ap_re[0] = pr
    ap_im[0] = pi
    bbr, bbi = bbr_s[...], bbi_s[...]
    bpr = pr * bbr - pi * bbi
    bpi = pr * bbi + pi * bbr
    bp_rr[0] = jnp.concatenate([bpr, bpr], axis=-1)
    bp_ii[0] = jnp.concatenate([bpi, bpi], axis=-1)
    cr, ci = cre_ref[...], cim_ref[...]
    cpr = pr * cr - pi * ci
    cpi = -(pr * ci + pi * cr)
    cp_rr[0] = jnp.concatenate([cpr, cpr], axis=-1)
    cp_ii[0] = jnp.concatenate([cpi, cpi], axis=-1)
    dn = (((2,), (2,)), ((0,), (0,)))
    hp = lax.Precision.HIGHEST
    km_ref[0] = (lax.dot_general(bpr, cr, dn, precision=hp, preferred_element_type=F32)
                 - lax.dot_general(bpi, ci, dn, precision=hp, preferred_element_type=F32))


def _ssm_params(lre, lim, log_dt, b_re, b_im, c_re, c_im):
    g, p, c = N_SSM_GROUPS, SSM_P, SSM_GROUP
    bc = lambda a: jnp.broadcast_to(a[:, None, :], (g, c, p))
    ldt = jnp.broadcast_to(log_dt[:, None, None], (g, c, p))
    bt_re = jnp.swapaxes(b_re, 1, 2)
    bt_im = jnp.swapaxes(b_im, 1, 2)
    gcp = pl.BlockSpec((g, c, p), lambda k: (0, 0, 0))
    o_gcp = pl.BlockSpec((1, g, c, p), lambda k: (k, 0, 0, 0))
    o_wide = pl.BlockSpec((1, g, c, 2 * p), lambda k: (k, 0, 0, 0))
    narrow = jax.ShapeDtypeStruct((SSM_NK, g, c, p), F32)
    wide = jax.ShapeDtypeStruct((SSM_NK, g, c, 2 * p), F32)
    ap_re, ap_im, bp_rr, bp_ii, cp_rr, cp_ii, km = pl.pallas_call(
        _ssm_param_body,
        grid=(SSM_NK,),
        in_specs=[gcp] * 7,
        out_specs=[o_gcp] * 2 + [o_wide] * 4 + [pl.BlockSpec((1, g, c, c), lambda k: (k, 0, 0, 0))],
        out_shape=[narrow] * 2 + [wide] * 4 + [jax.ShapeDtypeStruct((SSM_NK, g, c, c), F32)],
        scratch_shapes=[pltpu.VMEM((g, c, p), F32)] * 6,
        compiler_params=_cparams(("arbitrary",)),
        name="ssm_params",
    )(bc(lre), bc(lim), ldt, bt_re, bt_im, c_re, c_im)
    flat = lambda a: a.reshape(SSM_NK, g * c, 2 * p)
    return (ap_re[:, :, 0, :].reshape(SSM_NK, g * p), ap_im[:, :, 0, :].reshape(SSM_NK, g * p),
            flat(bp_rr), flat(bp_ii), flat(cp_rr), flat(cp_ii), km.reshape(SSM_NK, g * c, c))


def _gather_chunks(u_ref, length, m):
    xs = [u_ref[pl.ds(s, m, stride=length), :].astype(BF16) for s in range(length)]
    return jnp.concatenate(xs, axis=1)


SSM_SQ = 2 * SSM_SW // LANES
SSM_HQ = SSM_SQ // 2


def _ssm_body(u_ref, km_ref, bpr_ref, bpi_ref, cpr_ref, cpi_ref, are_ref, aim_ref, h0r_ref, h0i_ref,
              d_ref, *rest, length, m, n_seq, n_chunks, in_place):
    if in_place:
        rest = rest[1:]
    o_ref, finr_ref, fini_ref, v_s, ws_s, wct_s, s_s, hp_s = rest
    x = _gather_chunks(u_ref, length, m)

    erow = lax.broadcasted_iota(jnp.int32, (SSM_GROUP, LANES), 0)
    elane = lax.broadcasted_iota(jnp.int32, (SSM_GROUP, LANES), 1)
    spread = jnp.where(elane % SSM_GROUP == erow, 1.0, 0.0).astype(BF16)
    rgrp = lax.broadcasted_iota(jnp.int32, (LANES, LANES), 0) // SSM_GROUP
    lane = lax.broadcasted_iota(jnp.int32, (LANES, LANES), 1)
    for q in range(length):
        blk = jnp.dot(km_ref[length - 1 - q].astype(BF16), spread, preferred_element_type=F32)
        v_s[q * LANES:(q + 1) * LANES, :] = jnp.where(rgrp == lane // SSM_GROUP, blk, 0.0).astype(BF16)
    for s in range(length):
        rows = slice(s * LANES, (s + 1) * LANES)
        for q in range(SSM_HQ):
            own = rgrp == 2 * q + lane // SSM_P
            re_t = slice(q * LANES, (q + 1) * LANES)
            im_t = slice((SSM_HQ + q) * LANES, (SSM_HQ + q + 1) * LANES)
            ws_s[rows, re_t] = jnp.where(own, bpr_ref[length - 1 - s], 0.0).astype(BF16)
            ws_s[rows, im_t] = jnp.where(own, bpi_ref[length - 1 - s], 0.0).astype(BF16)
            wct_s[rows, re_t] = jnp.where(own, cpr_ref[s + 1], 0.0).astype(BF16)
            wct_s[rows, im_t] = jnp.where(own, cpi_ref[s + 1], 0.0).astype(BF16)

    s_loc = jnp.dot(x, ws_s[...], preferred_element_type=F32)
    for q in range(SSM_SQ):
        s_s[q] = s_loc[:, q * LANES:(q + 1) * LANES]

    if m != n_seq * n_chunks:
        hp_s[...] = jnp.zeros(hp_s.shape, F32)
    ar = [are_ref[:, q * LANES:(q + 1) * LANES] for q in range(SSM_HQ)]
    ai = [aim_ref[:, q * LANES:(q + 1) * LANES] for q in range(SSM_HQ)]

    def advance(rows, hs):
        out_r, out_i = [], []
        for q in range(SSM_HQ):
            hr, hi = hs[q], hs[SSM_HQ + q]
            hp_s[q, rows, :] = hr
            hp_s[SSM_HQ + q, rows, :] = hi
            out_r.append(ar[q] * hr - ai[q] * hi + s_s[q, rows, :])
            out_i.append(ar[q] * hi + ai[q] * hr + s_s[SSM_HQ + q, rows, :])
        return tuple(out_r + out_i)

    hs = tuple([h0r_ref[:, q * LANES:(q + 1) * LANES] for q in range(SSM_HQ)]
               + [h0i_ref[:, q * LANES:(q + 1) * LANES] for q in range(SSM_HQ)])
    if n_chunks == 1:
        hs = advance(pl.ds(0, n_seq), hs)
    else:
        hs = lax.fori_loop(0, n_chunks,
                           lambda c, hs: advance(pl.ds(c, n_seq, stride=n_chunks), hs), hs)
    for q in range(SSM_HQ):
        finr_ref[:, q * LANES:(q + 1) * LANES] = hs[q]
        fini_ref[:, q * LANES:(q + 1) * LANES] = hs[SSM_HQ + q]

    hp = jnp.concatenate([hp_s[q].astype(BF16) for q in range(SSM_SQ)], axis=1)
    d = d_ref[...]
    for t in range(length):
        rows = pl.ds(t, m, stride=length)
        y = jnp.dot(x[:, :(t + 1) * LANES], v_s[(length - 1 - t) * LANES:, :],
                    preferred_element_type=F32)
        y = y + lax.dot_general(hp, wct_s[t * LANES:(t + 1) * LANES, :], NT_DIMS,
                                preferred_element_type=F32)
        o_ref[rows, :] = y + d * u_ref[rows, :]


def _ssm_apply(u_all, y_all, ops, a_re, a_im, h0_re, h0_im, dvec, *, length, m, n_seq, n_chunks,
               row_blk):
    in_place = y_all is not None
    bp_rr, bp_ii, cp_rr, cp_ii, km = ops
    rows = pl.BlockSpec((m * length, LANES), lambda j: (row_blk, j))
    comp = lambda last: pl.BlockSpec((SSM_NK, LANES, last), lambda j: (0, j, 0))
    vec = pl.BlockSpec((1, SSM_SW), lambda j: (0, j))
    st = pl.BlockSpec((n_seq, SSM_SW), lambda j: (0, j))
    in_specs = [rows, comp(SSM_GROUP)] + [comp(LANES)] * 4 + [vec, vec, st, st,
                pl.BlockSpec((1, LANES), lambda j: (0, j))]
    args = [u_all, km, bp_rr, bp_ii, cp_rr, cp_ii, a_re, a_im, h0_re, h0_im, dvec]
    if in_place:
        in_specs.append(pl.BlockSpec(memory_space=pl.ANY))
        args.append(y_all)
    return pl.pallas_call(
        functools.partial(_ssm_body, length=length, m=m, n_seq=n_seq, n_chunks=n_chunks,
                          in_place=in_place),
        grid=(SSM_NJ,),
        in_specs=in_specs,
        out_specs=[rows, st, st],
        out_shape=[jax.ShapeDtypeStruct((ROWS, D_MODEL), F32),
                   jax.ShapeDtypeStruct((n_seq, N_SSM_GROUPS * SSM_P), F32),
                   jax.ShapeDtypeStruct((n_seq, N_SSM_GROUPS * SSM_P), F32)],
        input_output_aliases={len(args) - 1: 0} if in_place else {},
        scratch_shapes=[pltpu.VMEM((length * LANES, LANES), BF16),
                        pltpu.VMEM((length * LANES, SSM_SQ * LANES), BF16),
                        pltpu.VMEM((length * LANES, SSM_SQ * LANES), BF16),
                        pltpu.VMEM((SSM_SQ, m, LANES), F32),
                        pltpu.VMEM((SSM_SQ, m, LANES), F32)],
        compiler_params=_cparams(("parallel",)),
        name="ssm_sample" if in_place else "ssm_prompt",
    )(*args)


def kernel(x_prompt, x_sample, cache_k, cache_v, state_conv, state_ssm_re, state_ssm_im, page_table, meta_tokens, norm_mix_pre, norm_mix_post, norm_ffn_pre, norm_ffn_post, w_in_even, lambda_q, lambda_k, subln_g, conv_w, conv_b, conv_ln_g, conv_ln_b, w_out_even, w_in_odd, ssm_a_re, ssm_a_im, ssm_b_re, ssm_b_im, ssm_c_re, ssm_c_im, ssm_d, ssm_log_dt, w_glu, w_out_odd, w_ffn_gate, w_ffn_up, w_ffn_down):
    n_pool = cache_k.shape[1]
    meta = jnp.broadcast_to(meta_tokens.astype(F32)[None], (BATCH, N_META, D_MODEL))
    x = jnp.concatenate([
        jnp.concatenate([meta, x_prompt], axis=1).reshape(ROWS_P, D_MODEL),
        x_sample.reshape(ROWS_S, D_MODEL),
        jnp.zeros((ROWS - ROWS_P - ROWS_S, D_MODEL), F32)], axis=0)
    slopes_vec = jnp.array([2.0 ** (-8.0 * (i + 1) / N_DH) for i in range(N_DH)], dtype=F32)
    slopes = jnp.broadcast_to(slopes_vec[:, None, None], (N_DH, 1, LANES))
    sample = slice(ROWS_P, ROWS_P + ROWS_S)

    ks, vs, convs_p, convs_s, sre_p, sim_p, sre_s, sim_s = [], [], [], [], [], [], [], []
    for layer in range(DEPTH):
        if layer % 2 == 0:
            e = layer // 2
            lam_init = 0.8 - 0.6 * math.exp(-0.3 * (2 * e))
            w_in = w_in_even[e].astype(BF16)
            gpre = norm_mix_pre[layer]
            q_all = _norm_mm(x, gpre, [w_in[:, 0:D_ATT]])
            k_all = _norm_mm(x, gpre, [w_in[:, D_ATT:2 * D_ATT]])
            v_all = _norm_mm(x, gpre, [w_in[:, 2 * D_ATT:3 * D_ATT]])
            g_all = _norm_mm(x, gpre, [w_in[:, 3 * D_ATT:3 * D_ATT + D_CONV],
                                       w_in[:, 3 * D_ATT + D_CONV:]])
            o_all = _prompt_attn(q_all, k_all, v_all, slopes, lambda_q[e], lambda_k[e], subln_g[e],
                                 lam_init)
            o_s = _decode_attn(page_table,
                               q_all[sample].reshape(DEC_BATCH, DEC_SEQ, D_ATT),
                               k_all[sample].reshape(DEC_BATCH, DEC_SEQ, D_ATT),
                               v_all[sample].reshape(DEC_BATCH, DEC_SEQ, D_ATT),
                               jnp.transpose(cache_k[e], (0, 2, 3, 4, 1)).reshape(n_pool, D_ATT, PAGE_SIZE),
                               cache_v[e].reshape(n_pool, PAGE_SIZE * N_DH, DV),
                               slopes_vec, lambda_q[e], lambda_k[e], subln_g[e], lam_init)
            tail = jnp.zeros((ROWS - ROWS_P - ROWS_S, D_ATT), F32)
            o_all = lax.dynamic_update_slice(
                o_all, jnp.concatenate([o_s.reshape(ROWS_S, D_ATT), tail], axis=0), (ROWS_P, 0))
            g_s = g_all[sample].reshape(DEC_BATCH, DEC_SEQ, D_CONV)
            ext_s = jnp.concatenate([state_conv[e], g_s], axis=1)
            c_all = _conv_prompt(g_all, jnp.zeros((BATCH, CONV_PRE, D_CONV), F32), conv_w[e], conv_b[e])
            c_s = _conv_sample(ext_s, conv_w[e], conv_b[e])
            c_all = lax.dynamic_update_slice(c_all, jnp.concatenate([c_s, tail], axis=0), (ROWS_P, 0))
            w_out = w_out_even[e].astype(BF16)
            x = _even_out(o_all, c_all, conv_ln_g[e], conv_ln_b[e], w_out[:D_ATT], w_out[D_ATT:],
                          norm_mix_post[layer], x)
            ks.append((k_all[:ROWS_P].reshape(BATCH, T_P, N_DH, 2, DK),
                       k_all[sample].reshape(DEC_BATCH, DEC_SEQ, N_DH, 2, DK)))
            vs.append((v_all[:ROWS_P].reshape(BATCH, T_P, N_DH, DV),
                       v_all[sample].reshape(DEC_BATCH, DEC_SEQ, N_DH, DV)))
            convs_p.append(g_all[:ROWS_P].reshape(BATCH, T_P, D_CONV)[:, T_P - (CONV_W - 1):])
            convs_s.append(ext_s[:, DEC_SEQ:])
        else:
            o = layer // 2
            u_all = _norm_mm(x, norm_mix_pre[layer], [w_in_odd[o].astype(BF16)])
            gp = N_SSM_GROUPS * SSM_P
            ap_re, ap_im, *ops = _ssm_params(
                ssm_a_re[o], ssm_a_im[o], ssm_log_dt[o], ssm_b_re[o], ssm_b_im[o],
                ssm_c_re[o], ssm_c_im[o])
            dvec = ssm_d[o].reshape(1, D_MODEL)
            zeros = jnp.zeros((BATCH, gp), F32)
            y_all, pr, pi = _ssm_apply(
                u_all, None, ops, ap_re[SSM_L:SSM_L + 1], ap_im[SSM_L:SSM_L + 1], zeros, zeros, dvec,
                length=SSM_L, m=SSM_M, n_seq=BATCH, n_chunks=SSM_NC, row_blk=0)
            y_all, sr, si = _ssm_apply(
                u_all, y_all, ops, ap_re[DEC_SEQ:DEC_SEQ + 1], ap_im[DEC_SEQ:DEC_SEQ + 1],
                state_ssm_re[o].reshape(DEC_BATCH, gp), state_ssm_im[o].reshape(DEC_BATCH, gp), dvec,
                length=DEC_SEQ, m=DEC_BATCH, n_seq=DEC_BATCH, n_chunks=1, row_blk=SAMPLE_BLK)
            x = _odd_out(y_all, x, w_glu[o].astype(BF16), w_out_odd[o].astype(BF16),
                         norm_mix_post[layer])
            sre_p.append(pr.reshape(BATCH, N_SSM_GROUPS, SSM_P))
            sim_p.append(pi.reshape(BATCH, N_SSM_GROUPS, SSM_P))
            sre_s.append(sr.reshape(DEC_BATCH, N_SSM_GROUPS, SSM_P))
            sim_s.append(si.reshape(DEC_BATCH, N_SSM_GROUPS, SSM_P))
        x = _ffn(x, norm_ffn_pre[layer], w_ffn_gate[layer].astype(BF16), w_ffn_up[layer].astype(BF16),
                 w_ffn_down[layer].astype(BF16), norm_ffn_post[layer])

    y_prompt = x[:ROWS_P].reshape(BATCH, T_P, D_MODEL)[:, N_META:]
    y_sample = x[sample].reshape(DEC_BATCH, DEC_SEQ, D_MODEL)
    return (y_prompt, y_sample,
            jnp.stack([k[0] for k in ks]), jnp.stack([v[0] for v in vs]),
            jnp.stack([k[1] for k in ks]), jnp.stack([v[1] for v in vs]),
            jnp.stack(convs_p), jnp.stack(convs_s),
            jnp.stack(sre_p), jnp.stack(sim_p), jnp.stack(sre_s), jnp.stack(sim_s))
```

```python
import functools
import math

import jax
import jax.numpy as jnp
from jax import lax
from jax.experimental import pallas as pl
from jax.experimental.pallas import tpu as pltpu

F32 = jnp.float32
BF16 = jnp.bfloat16

D_MODEL = 2048
BATCH = 4
SEQ = 2048
DEPTH = 2
DEC_BATCH = 8
DEC_SEQ = 4
PAGE_SIZE = 128
N_META = 16
D_ATT = D_MODEL // 2
N_DH = 8
DK = D_ATT // N_DH // 2
DV = 2 * DK
D_CONV = D_MODEL - D_ATT
CONV_W = 31
SSM_GROUP = 16
N_SSM_GROUPS = D_MODEL // SSM_GROUP
SSM_P = 64
D_FF = ((8 * D_MODEL // 3 + 255) // 256) * 256
EPS = 1e-6

T_P = N_META + SEQ
ROWS_P = BATCH * T_P
ROWS_S = DEC_BATCH * DEC_SEQ
TM = 640
ROWS = 8320
assert ROWS % TM == 0 and ROWS >= ROWS_P + ROWS_S
assert ROWS_P % ROWS_S == 0
SAMPLE_BLK = ROWS_P // ROWS_S

LANES = 128
SUBLANES = 8
VMEM_LIMIT = 56 * 1024 * 1024

SSM_L = 16
SSM_NC = T_P // SSM_L
SSM_M = ROWS // SSM_L
SSM_NJ = D_MODEL // LANES
SSM_G8 = LANES // SSM_GROUP
SSM_SW = SSM_G8 * SSM_P
assert T_P % SSM_L == 0 and ROWS % SSM_L == 0

ATT_T = 256
ATT_NT = SEQ // ATT_T
LOG2E = 1.4426950408889634
NT_DIMS = (((1,), (1,)), ((), ()))
DEC_PP = 8


def _cparams(sem, vmem=VMEM_LIMIT):
    return pltpu.CompilerParams(dimension_semantics=sem, vmem_limit_bytes=vmem)


def _rms(x, g):
    ms = jnp.mean(x * x, axis=-1, keepdims=True)
    return x * lax.rsqrt(ms + EPS) * g


def _norm_mm_body(x_ref, g_ref, *refs, glu):
    n_w = 2 if glu else 1
    w_refs, o_ref, hb = refs[:n_w], refs[n_w], refs[n_w + 1]

    @pl.when(pl.program_id(1) == 0)
    def _():
        hb[...] = _rms(x_ref[...], g_ref[...]).astype(BF16)

    h = hb[...]
    a = jnp.dot(h, w_refs[0][...], preferred_element_type=F32)
    if glu:
        b = jnp.dot(h, w_refs[1][...], preferred_element_type=F32)
        a = a * jax.nn.sigmoid(b)
    o_ref[...] = a


def _norm_mm(x, g, ws, *, tn=512):
    rows, d = x.shape
    n = ws[0].shape[1]
    glu = len(ws) == 2
    return pl.pallas_call(
        functools.partial(_norm_mm_body, glu=glu),
        grid=(rows // TM, n // tn),
        in_specs=[pl.BlockSpec((TM, d), lambda i, j: (i, 0)),
                  pl.BlockSpec((1, d), lambda i, j: (0, 0))]
                 + [pl.BlockSpec((d, tn), lambda i, j: (0, j)) for _ in ws],
        out_specs=pl.BlockSpec((TM, tn), lambda i, j: (i, j)),
        out_shape=jax.ShapeDtypeStruct((rows, n), F32),
        scratch_shapes=[pltpu.VMEM((TM, d), BF16)],
        compiler_params=_cparams(("parallel", "arbitrary")),
        name="norm_mm_glu" if glu else "norm_mm",
    )(x, g.reshape(1, d), *ws)


def _rms_cast_body(x_ref, g_ref, o_ref):
    o_ref[...] = _rms(x_ref[...], g_ref[...]).astype(BF16)


def _rms_cast(x, g):
    rows, d = x.shape
    return pl.pallas_call(
        _rms_cast_body,
        grid=(rows // TM,),
        in_specs=[pl.BlockSpec((TM, d), lambda i: (i, 0)), pl.BlockSpec((1, d), lambda i: (0, 0))],
        out_specs=pl.BlockSpec((TM, d), lambda i: (i, 0)),
        out_shape=jax.ShapeDtypeStruct((rows, d), BF16),
        compiler_params=_cparams(("parallel",)),
        name="rms_cast",
    )(x, g.reshape(1, d))


def _mm_body(h_ref, *refs, glu):
    h = h_ref[...]
    a = jnp.dot(h, refs[0][...], preferred_element_type=F32)
    if glu:
        a = a * jax.nn.sigmoid(jnp.dot(h, refs[1][...], preferred_element_type=F32))
    refs[-1][...] = a


def _mm(h, ws, *, tn=512):
    rows, d = h.shape
    n = ws[0].shape[1]
    glu = len(ws) == 2
    return pl.pallas_call(
        functools.partial(_mm_body, glu=glu),
        grid=(rows // TM, n // tn),
        in_specs=[pl.BlockSpec((TM, d), lambda i, j: (i, 0))]
                 + [pl.BlockSpec((d, tn), lambda i, j: (0, j)) for _ in ws],
        out_specs=pl.BlockSpec((TM, tn), lambda i, j: (i, j)),
        out_shape=jax.ShapeDtypeStruct((rows, n), F32),
        compiler_params=_cparams(("parallel", "arbitrary")),
        name="mm_glu" if glu else "mm",
    )(h, *ws)


def _ffn_body(x_ref, gpre_ref, wg_ref, wu_ref, wd_ref, gpost_ref, o_ref, hb, acc):
    j = pl.program_id(1)

    @pl.when(j == 0)
    def _():
        hb[...] = _rms(x_ref[...], gpre_ref[...]).astype(BF16)

    h = hb[...]
    gate = jnp.dot(h, wg_ref[...], preferred_element_type=F32)
    up = jnp.dot(h, wu_ref[...], preferred_element_type=F32)
    a = (jax.nn.silu(gate) * up).astype(BF16)
    part = jnp.dot(a, wd_ref[...], preferred_element_type=F32)

    @pl.when(j == 0)
    def _():
        acc[...] = part

    @pl.when(j > 0)
    def _():
        acc[...] += part

    @pl.when(j == pl.num_programs(1) - 1)
    def _():
        o_ref[...] = x_ref[...] + _rms(acc[...], gpost_ref[...])


def _ffn(x, gpre, wg, wu, wd, gpost, *, tf=512):
    rows, d = x.shape
    dff = wg.shape[1]
    return pl.pallas_call(
        _ffn_body,
        grid=(rows // TM, dff // tf),
        in_specs=[pl.BlockSpec((TM, d), lambda i, j: (i, 0)),
                  pl.BlockSpec((1, d), lambda i, j: (0, 0)),
                  pl.BlockSpec((d, tf), lambda i, j: (0, j)),
                  pl.BlockSpec((d, tf), lambda i, j: (0, j)),
                  pl.BlockSpec((tf, d), lambda i, j: (j, 0)),
                  pl.BlockSpec((1, d), lambda i, j: (0, 0))],
        out_specs=pl.BlockSpec((TM, d), lambda i, j: (i, 0)),
        out_shape=jax.ShapeDtypeStruct((rows, d), F32),
        scratch_shapes=[pltpu.VMEM((TM, d), BF16), pltpu.VMEM((TM, d), F32)],
        compiler_params=_cparams(("parallel", "arbitrary")),
        name="ffn",
    )(x, gpre.reshape(1, d), wg, wu, wd, gpost.reshape(1, d))


def _odd_out_body(y_ref, res_ref, wglu_ref, wout_ref, gpost_ref, o_ref, yf, hb, acc, *, tf):
    j = pl.program_id(1)
    nj = pl.num_programs(1)

    @pl.when(j == 0)
    def _():
        for jj in range(yf.shape[0]):
            gy = jax.nn.gelu(y_ref[:, jj * tf:(jj + 1) * tf])
            yf[jj] = gy
            hb[:, jj * tf:(jj + 1) * tf] = gy.astype(BF16)

    t = jnp.dot(hb[...], wglu_ref[...], preferred_element_type=F32)
    a = (yf[j] * jax.nn.sigmoid(t)).astype(BF16)
    part = jnp.dot(a, wout_ref[...], preferred_element_type=F32)

    @pl.when(j == 0)
    def _():
        acc[...] = part

    @pl.when(j > 0)
    def _():
        acc[...] += part

    @pl.when(j == nj - 1)
    def _():
        o_ref[...] = res_ref[...] + _rms(acc[...], gpost_ref[...])


def _odd_out(y, res, wglu, wout, gpost, *, tf=512, tm=TM):
    rows, d = y.shape
    once = pl.Buffered(1)
    return pl.pallas_call(
        functools.partial(_odd_out_body, tf=tf),
        grid=(rows // tm, d // tf),
        in_specs=[pl.BlockSpec((tm, d), lambda i, j: (i, 0), pipeline_mode=once),
                  pl.BlockSpec((tm, d), lambda i, j: (i, 0), pipeline_mode=once),
                  pl.BlockSpec((d, tf), lambda i, j: (0, j)),
                  pl.BlockSpec((tf, d), lambda i, j: (j, 0)),
                  pl.BlockSpec((1, d), lambda i, j: (0, 0))],
        out_specs=pl.BlockSpec((tm, d), lambda i, j: (i, 0)),
        out_shape=jax.ShapeDtypeStruct((rows, d), F32),
        scratch_shapes=[pltpu.VMEM((d // tf, tm, tf), F32), pltpu.VMEM((tm, d), BF16),
                        pltpu.VMEM((tm, d), F32)],
        compiler_params=_cparams(("parallel", "arbitrary")),
        name="odd_out",
    )(y, res, wglu, wout, gpost.reshape(1, d))


def _even_out_body(o_ref, c_ref, lng_ref, lnb_ref, wtop_ref, wbot_ref, gpost_ref, res_ref, out_ref):
    c = c_ref[...]
    mu = jnp.mean(c, axis=-1, keepdims=True)
    var = jnp.mean(jnp.square(c - mu), axis=-1, keepdims=True)
    cn = jax.nn.silu((c - mu) * lax.rsqrt(var + EPS) * lng_ref[...] + lnb_ref[...])
    y = jnp.dot(o_ref[...].astype(BF16), wtop_ref[...], preferred_element_type=F32)
    y = y + jnp.dot(cn.astype(BF16), wbot_ref[...], preferred_element_type=F32)
    out_ref[...] = res_ref[...] + _rms(y, gpost_ref[...])


def _even_out(o, c, lng, lnb, wtop, wbot, gpost, res):
    rows, d = res.shape
    da, dc = o.shape[1], c.shape[1]
    return pl.pallas_call(
        _even_out_body,
        grid=(rows // TM,),
        in_specs=[pl.BlockSpec((TM, da), lambda i: (i, 0)),
                  pl.BlockSpec((TM, dc), lambda i: (i, 0)),
                  pl.BlockSpec((1, dc), lambda i: (0, 0)),
                  pl.BlockSpec((1, dc), lambda i: (0, 0)),
                  pl.BlockSpec((da, d), lambda i: (0, 0)),
                  pl.BlockSpec((dc, d), lambda i: (0, 0)),
                  pl.BlockSpec((1, d), lambda i: (0, 0)),
                  pl.BlockSpec((TM, d), lambda i: (i, 0))],
        out_specs=pl.BlockSpec((TM, d), lambda i: (i, 0)),
        out_shape=jax.ShapeDtypeStruct((rows, d), F32),
        compiler_params=_cparams(("parallel",)),
        name="even_out",
    )(o, c, lng.reshape(1, dc), lnb.reshape(1, dc), wtop, wbot, gpost.reshape(1, d), res)


def _lam_from(lq_ref, lk_ref, lam_init):
    s0 = jnp.sum(lq_ref[0:1, :] * lk_ref[0:1, :], axis=-1, keepdims=True)
    s1 = jnp.sum(lq_ref[1:2, :] * lk_ref[1:2, :], axis=-1, keepdims=True)
    return jnp.exp(s0) - jnp.exp(s1) + lam_init


def _softmax_step(carry, q2, kc, vc, bias, k_is_transposed=False):
    m, l, acc = carry
    if k_is_transposed:
        s = jnp.dot(q2, kc, preferred_element_type=F32) + bias
    else:
        s = lax.dot_general(q2, kc, (((1,), (1,)), ((), ())), preferred_element_type=F32) + bias
    m_new = jnp.maximum(m, jnp.max(s, axis=-1, keepdims=True))
    alpha = jnp.exp(m - m_new)
    p = jnp.exp(s - m_new)
    l = alpha * l + jnp.sum(p, axis=-1, keepdims=True)
    acc = alpha * acc + jnp.dot(p.astype(BF16), vc, preferred_element_type=F32)
    return m_new, l, acc


def _stack_maps(q, scale):
    lane = lax.broadcasted_iota(jnp.int32, q.shape, 1)
    qs = q * scale
    q0 = jnp.where(lane < DK, qs, 0.0)
    q1 = jnp.where(lane >= DK, qs, 0.0)
    return jnp.concatenate([q0, q1], axis=0).astype(BF16)


def _diff_out(m, l, acc, n, lam, sg, lam_init):
    o = acc[:n] / l[:n] - lam * (acc[n:] / l[n:])
    return _rms(o, sg) * (1.0 - lam_init)


def _prompt_attn_body(slope_ref, lq_ref, lk_ref, sg_ref, q_ref, k_ref, v_ref, o_ref,
                      kb, vb, nb_s, *, lam_init):
    slope = slope_ref[0, 0:1, 0:1]
    lam = _lam_from(lq_ref, lk_ref, lam_init)
    sg = sg_ref[...]
    scale = DK ** -0.5
    kb[...] = k_ref[...].astype(BF16)
    vb[:, 0:DV] = v_ref[...].astype(BF16)
    neg_inf = float("-inf")

    zpad = jnp.zeros((LANES - N_META, LANES), BF16)
    k_meta = jnp.concatenate([kb[0:N_META, :], zpad], axis=0)
    v_meta = jnp.concatenate([vb[0:N_META, 0:DV], zpad], axis=0)

    rm = lax.broadcasted_iota(jnp.int32, (2 * N_META, LANES), 0) % N_META
    cm = lax.broadcasted_iota(jnp.int32, (2 * N_META, LANES), 1)
    bias_m = jnp.where(cm <= rm, -slope * (rm - cm).astype(F32), neg_inf)
    q2 = _stack_maps(q_ref[0:N_META, :], scale)
    init = (jnp.full((2 * N_META, 1), neg_inf, F32), jnp.zeros((2 * N_META, 1), F32),
            jnp.zeros((2 * N_META, LANES), F32))
    m, l, acc = _softmax_step(init, q2, k_meta, v_meta, bias_m)
    o_ref[0:N_META, :] = _diff_out(m, l, acc, N_META, lam, sg, lam_init)

    sl2 = slope * LOG2E
    vb[:, DV:2 * DV] = jnp.ones((T_P, DV), BF16)
    rr = lax.broadcasted_iota(jnp.int32, (2 * ATT_T, ATT_T), 0) % ATT_T
    cc = lax.broadcasted_iota(jnp.int32, (2 * ATT_T, ATT_T), 1)
    nbase = -sl2 * (rr - cc).astype(F32)
    nb_s[0] = nbase
    nb_s[1] = jnp.where(cc < N_META, nbase, neg_inf)
    nb_s[2] = jnp.where(cc <= rr, nbase, neg_inf)

    for i in range(ATT_NT):
        q0 = N_META + i * ATT_T
        q2 = _stack_maps(q_ref[q0:q0 + ATT_T, :], scale * LOG2E)
        tiles = [(0, 1, float(N_META + i * ATT_T))]
        tiles += [(N_META + j * ATT_T, 0, float((i - j) * ATT_T)) for j in range(i)]
        tiles += [(q0, 2, 0.0)]
        scores, mx = [], jnp.full((2 * ATT_T, LANES), neg_inf, F32)
        for k0, kind, off in tiles:
            s = lax.dot_general(q2, kb[k0:k0 + ATT_T, :], NT_DIMS, preferred_element_type=F32)
            s = s + nb_s[kind]
            mx = jnp.maximum(mx, jnp.maximum(s[:, 0:LANES], s[:, LANES:2 * LANES]) - sl2 * off)
            scores.append(s)
        m = jnp.max(mx, axis=-1, keepdims=True)
        probs = []
        for (k0, kind, off), s in zip(tiles, scores):
            probs.append(jnp.exp2(s - (m + sl2 * off)).astype(BF16))
        p_all = jnp.concatenate(probs, axis=1)
        v_all = jnp.concatenate([vb[0:ATT_T, :], vb[N_META:q0 + ATT_T, :]], axis=0)
        acc = jnp.dot(p_all, v_all, preferred_element_type=F32)
        o0 = acc[0:ATT_T, 0:DV] / acc[0:ATT_T, DV:2 * DV]
        o1 = acc[ATT_T:2 * ATT_T, 0:DV] / acc[ATT_T:2 * ATT_T, DV:2 * DV]
        o_ref[q0:q0 + ATT_T, :] = _rms(o0 - lam * o1, sg) * (1.0 - lam_init)


def _prompt_attn(q_all, k_all, v_all, slopes, lq, lk, sg, lam_init):
    blk = pl.BlockSpec((T_P, DV), lambda b, h: (b, h))
    return pl.pallas_call(
        functools.partial(_prompt_attn_body, lam_init=lam_init),
        grid=(BATCH, N_DH),
        in_specs=[pl.BlockSpec((1, 1, LANES), lambda b, h: (h, 0, 0)),
                  pl.BlockSpec((2, DK), lambda b, h: (0, 0)),
                  pl.BlockSpec((2, DK), lambda b, h: (0, 0)),
                  pl.BlockSpec((1, DV), lambda b, h: (0, 0)),
                  blk, blk, blk],
        out_specs=blk,
        out_shape=jax.ShapeDtypeStruct((ROWS, D_ATT), F32),
        scratch_shapes=[pltpu.VMEM((T_P, DV), BF16),
                        pltpu.VMEM((T_P, 2 * DV), BF16),
                        pltpu.VMEM((3, 2 * ATT_T, ATT_T), F32)],
        compiler_params=_cparams(("parallel", "parallel")),
        name="prompt_attn",
    )(slopes, lq, lk, sg.reshape(1, DV), q_all, k_all, v_all)


def _decode_attn_body(pt_ref, srow_ref, qi_ref, lq_ref, lk_ref, sg_ref, q_ref, kn_ref, vn_ref, *refs,
                      lam_init, n_pages):
    k_refs, v_refs = refs[:DEC_PP], refs[DEC_PP:2 * DEC_PP]
    o_ref, m_s, l_s, acc_s = refs[2 * DEC_PP:]
    s_id = pl.program_id(1)
    n_rows = 2 * N_DH * DEC_SEQ
    past_len = n_pages * PAGE_SIZE
    neg_inf = float("-inf")

    row = lax.broadcasted_iota(jnp.int32, (n_rows, D_ATT), 0)
    lane = lax.broadcasted_iota(jnp.int32, (n_rows, D_ATT), 1)
    q2 = jnp.where(lane // DK == row // DEC_SEQ, q_ref[0] * (DK ** -0.5), 0.0).astype(BF16)
    srow = srow_ref[...]
    qi = qi_ref[...]
    col = lax.broadcasted_iota(jnp.int32, (n_rows, PAGE_SIZE), 1).astype(F32)
    srow_w = jnp.concatenate([srow] * DEC_PP, axis=1)
    qi_w = jnp.concatenate([qi] * DEC_PP, axis=1)
    col_w = lax.broadcasted_iota(jnp.int32, (n_rows, DEC_PP * PAGE_SIZE), 1).astype(F32)

    @pl.when(s_id == 0)
    def _():
        m_s[...] = jnp.full(m_s.shape, neg_inf, F32)
        l_s[...] = jnp.zeros(l_s.shape, F32)
        acc_s[...] = jnp.zeros(acc_s.shape, F32)

    def step(kc, vc, bias, k_is_transposed=False):
        m, l, acc = _softmax_step((m_s[:, 0:1], l_s[:, 0:1], acc_s[...]), q2, kc, vc, bias,
                                  k_is_transposed)
        m_s[...] = jnp.broadcast_to(m, m_s.shape)
        l_s[...] = jnp.broadcast_to(l, l_s.shape)
        acc_s[...] = acc

    kt = jnp.concatenate([k_refs[p][0].astype(BF16) for p in range(DEC_PP)], axis=1)
    vc = jnp.concatenate(
        [jnp.concatenate(
            [v_refs[p][0, pl.ds(h, PAGE_SIZE, stride=N_DH), :].astype(BF16) for h in range(N_DH)],
            axis=1) for p in range(DEC_PP)], axis=0)
    kpos0 = jnp.asarray(s_id * (DEC_PP * PAGE_SIZE), F32)
    dist = (past_len + qi_w) - (kpos0 + col_w)
    step(kt, vc, -srow_w * dist, k_is_transposed=True)

    @pl.when(s_id == pl.num_programs(1) - 1)
    def _():
        dist = qi - col
        bias = jnp.where(dist >= 0, -srow * dist, neg_inf)
        step(kn_ref[0].astype(BF16), vn_ref[0].astype(BF16), bias)
        lam = _lam_from(lq_ref, lk_ref, lam_init)
        sg = sg_ref[...]
        l = l_s[:, 0:1]
        for h in range(N_DH):
            blk = acc_s[h * 2 * DEC_SEQ:(h + 1) * 2 * DEC_SEQ, h * DV:(h + 1) * DV]
            blk = blk / l[h * 2 * DEC_SEQ:(h + 1) * 2 * DEC_SEQ]
            o = blk[0:DEC_SEQ] - lam * blk[DEC_SEQ:2 * DEC_SEQ]
            o_ref[0, :, h * DV:(h + 1) * DV] = _rms(o, sg) * (1.0 - lam_init)


def _decode_attn(page_table, q_s, k_s, v_s, cache_k, cache_v, slopes_vec, lq, lk, sg, lam_init):
    n_pages = page_table.shape[1]
    n_rows = 2 * N_DH * DEC_SEQ
    q_t = jnp.tile(q_s, (1, 2 * N_DH, 1))
    pad = ((0, 0), (0, PAGE_SIZE - DEC_SEQ), (0, 0))
    kn = jnp.pad(k_s, pad)
    vn = jnp.pad(v_s, pad)
    ridx = jnp.arange(n_rows)
    srow = jnp.broadcast_to(slopes_vec[ridx // (2 * DEC_SEQ)][:, None], (n_rows, PAGE_SIZE)).astype(F32)
    qi = jnp.broadcast_to((ridx % DEC_SEQ)[:, None], (n_rows, PAGE_SIZE)).astype(F32)

    def page_spec(p, shape):
        return pl.BlockSpec((1,) + shape, lambda b, s, pt: (pt[b, s * DEC_PP + p], 0, 0))

    const2 = lambda b, s, pt: (0, 0)
    per_b = lambda b, s, pt: (b, 0, 0)
    grid_spec = pltpu.PrefetchScalarGridSpec(
        num_scalar_prefetch=1,
        grid=(DEC_BATCH, n_pages // DEC_PP),
        in_specs=[pl.BlockSpec((n_rows, PAGE_SIZE), const2),
                  pl.BlockSpec((n_rows, PAGE_SIZE), const2),
                  pl.BlockSpec((2, DK), const2),
                  pl.BlockSpec((2, DK), const2),
                  pl.BlockSpec((1, DV), const2),
                  pl.BlockSpec((1, n_rows, D_ATT), per_b),
                  pl.BlockSpec((1, PAGE_SIZE, D_ATT), per_b),
                  pl.BlockSpec((1, PAGE_SIZE, D_ATT), per_b)]
                 + [page_spec(p, (D_ATT, PAGE_SIZE)) for p in range(DEC_PP)]
                 + [page_spec(p, (PAGE_SIZE * N_DH, DV)) for p in range(DEC_PP)],
        out_specs=pl.BlockSpec((1, DEC_SEQ, D_ATT), per_b),
        scratch_shapes=[pltpu.VMEM((n_rows, LANES), F32), pltpu.VMEM((n_rows, LANES), F32),
                        pltpu.VMEM((n_rows, D_ATT), F32)],
    )
    return pl.pallas_call(
        functools.partial(_decode_attn_body, lam_init=lam_init, n_pages=n_pages),
        grid_spec=grid_spec,
        out_shape=jax.ShapeDtypeStruct((DEC_BATCH, DEC_SEQ, D_ATT), F32),
        compiler_params=_cparams(("parallel", "arbitrary")),
        name="decode_attn",
    )(page_table, srow, qi, lq, lk, sg.reshape(1, DV), q_t, kn, vn,
      *([cache_k] * DEC_PP), *([cache_v] * DEC_PP))


CONV_PRE = 32
CONV_RC = 48
CONV_CT = 256
assert T_P % CONV_RC == 0


def _conv_prompt_body(prev_ref, g_ref, w_ref, b_ref, o_ref, gp):
    gp[0:CONV_PRE, :] = prev_ref[0]
    gp[CONV_PRE:, :] = g_ref[...]
    lead = CONV_PRE - (CONV_W - 1)
    win_rows = CONV_RC + CONV_PRE
    bias = b_ref[...]

    def chunk(ci, _):
        t0 = pl.multiple_of(ci * CONV_RC, SUBLANES)
        win = gp[pl.ds(t0, win_rows), :]
        acc = jnp.broadcast_to(bias, (CONV_RC, CONV_CT))
        for r in range(SUBLANES):
            taps = [k for k in range(CONV_W) if (k + lead) % SUBLANES == r]
            if not taps:
                continue
            hi = max(k + lead for k in taps) - r + CONV_RC
            wr = win[r:r + hi, :]
            for k in taps:
                a = k + lead - r
                acc = acc + w_ref[k:k + 1, :] * wr[a:a + CONV_RC, :]
        o_ref[pl.ds(t0, CONV_RC), :] = acc
        return 0

    lax.fori_loop(0, T_P // CONV_RC, chunk, 0)


def _conv_prompt(g_all, prev, w, b):
    return pl.pallas_call(
        _conv_prompt_body,
        grid=(BATCH, D_CONV // CONV_CT),
        in_specs=[pl.BlockSpec((1, CONV_PRE, CONV_CT), lambda bi, j: (bi, 0, j)),
                  pl.BlockSpec((T_P, CONV_CT), lambda bi, j: (bi, j)),
                  pl.BlockSpec((CONV_W, CONV_CT), lambda bi, j: (0, j)),
                  pl.BlockSpec((1, CONV_CT), lambda bi, j: (0, j))],
        out_specs=pl.BlockSpec((T_P, CONV_CT), lambda bi, j: (bi, j)),
        out_shape=jax.ShapeDtypeStruct((ROWS, D_CONV), F32),
        scratch_shapes=[pltpu.VMEM((CONV_PRE + T_P, CONV_CT), F32)],
        compiler_params=_cparams(("parallel", "parallel")),
        name="conv_prompt",
    )(prev, g_all, w, b.reshape(1, D_CONV))


def _conv_sample_body(ext_ref, w_ref, b_ref, o_ref):
    w = w_ref[...]
    for bi in range(DEC_BATCH):
        for t in range(DEC_SEQ):
            acc = jnp.sum(w * ext_ref[bi, t:t + CONV_W, :], axis=0, keepdims=True) + b_ref[...]
            o_ref[bi * DEC_SEQ + t:bi * DEC_SEQ + t + 1, :] = acc


def _conv_sample(ext, w, b):
    return pl.pallas_call(
        _conv_sample_body,
        out_shape=jax.ShapeDtypeStruct((ROWS_S, D_CONV), F32),
        name="conv_sample",
    )(ext, w, b.reshape(1, D_CONV))


SSM_NK = SSM_L + 1


def _ssm_param_body(lre_ref, lim_ref, ldt_ref, bre_ref, bim_ref, cre_ref, cim_ref,
                    ap_re, ap_im, bp_rr, bp_ii, cp_rr, cp_ii, km_ref,
                    ar_s, ai_s, pr_s, pi_s, bbr_s, bbi_s):
    k = pl.program_id(0)

    @pl.when(k == 0)
    def _():
        lr, li = lre_ref[...], lim_ref[...]
        dt = jnp.exp(ldt_ref[...])
        er = jnp.exp(lr * dt)
        ar = er * jnp.cos(li * dt)
        ai = er * jnp.sin(li * dt)
        den = lr * lr + li * li
        xr, xi = ar - 1.0, ai
        cr = (xr * lr + xi * li) / den
        ci = (xi * lr - xr * li) / den
        br, bi = bre_ref[...], bim_ref[...]
        bbr_s[...] = cr * br - ci * bi
        bbi_s[...] = cr * bi + ci * br
        ar_s[...] = ar
        ai_s[...] = ai
        pr_s[...] = jnp.ones(pr_s.shape, F32)
        pi_s[...] = jnp.zeros(pi_s.shape, F32)

    @pl.when(k > 0)
    def _():
        pr, pi = pr_s[...], pi_s[...]
        ar, ai = ar_s[...], ai_s[...]
        pr_s[...] = pr * ar - pi * ai
        pi_s[...] = pr * ai + pi * ar

    pr, pi = pr_s[...], pi_s[...]
    ap_re[0] = pr
    ap_im[0] = pi
    bbr, bbi = bbr_s[...], bbi_s[...]
    bpr = pr * bbr - pi * bbi
    bpi = pr * bbi + pi * bbr
    bp_rr[0] = jnp.concatenate([bpr, bpr], axis=-1)
    bp_ii[0] = jnp.concatenate([bpi, bpi], axis=-1)
    cr, ci = cre_ref[...], cim_ref[...]
    cpr = pr * cr - pi * ci
    cpi = -(pr * ci + pi * cr)
    cp_rr[0] = jnp.concatenate([cpr, cpr], axis=-1)
    cp_ii[0] = jnp.concatenate([cpi, cpi], axis=-1)
    dn = (((2,), (2,)), ((0,), (0,)))
    hp = lax.Precision.HIGHEST
    km_ref[0] = (lax.dot_general(bpr, cr, dn, precision=hp, preferred_element_type=F32)
                 - lax.dot_general(bpi, ci, dn, precision=hp, preferred_element_type=F32))


def _ssm_params(lre, lim, log_dt, b_re, b_im, c_re, c_im):
    g, p, c = N_SSM_GROUPS, SSM_P, SSM_GROUP
    bc = lambda a: jnp.broadcast_to(a[:, None, :], (g, c, p))
    ldt = jnp.broadcast_to(log_dt[:, None, None], (g, c, p))
    bt_re = jnp.swapaxes(b_re, 1, 2)
    bt_im = jnp.swapaxes(b_im, 1, 2)
    gcp = pl.BlockSpec((g, c, p), lambda k: (0, 0, 0))
    o_gcp = pl.BlockSpec((1, g, c, p), lambda k: (k, 0, 0, 0))
    o_wide = pl.BlockSpec((1, g, c, 2 * p), lambda k: (k, 0, 0, 0))
    narrow = jax.ShapeDtypeStruct((SSM_NK, g, c, p), F32)
    wide = jax.ShapeDtypeStruct((SSM_NK, g, c, 2 * p), F32)
    ap_re, ap_im, bp_rr, bp_ii, cp_rr, cp_ii, km = pl.pallas_call(
        _ssm_param_body,
        grid=(SSM_NK,),
        in_specs=[gcp] * 7,
        out_specs=[o_gcp] * 2 + [o_wide] * 4 + [pl.BlockSpec((1, g, c, c), lambda k: (k, 0, 0, 0))],
        out_shape=[narrow] * 2 + [wide] * 4 + [jax.ShapeDtypeStruct((SSM_NK, g, c, c), F32)],
        scratch_shapes=[pltpu.VMEM((g, c, p), F32)] * 6,
        compiler_params=_cparams(("arbitrary",)),
        name="ssm_params",
    )(bc(lre), bc(lim), ldt, bt_re, bt_im, c_re, c_im)
    flat = lambda a: a.reshape(SSM_NK, g * c, 2 * p)
    return (ap_re[:, :, 0, :].reshape(SSM_NK, g * p), ap_im[:, :, 0, :].reshape(SSM_NK, g * p),
            flat(bp_rr), flat(bp_ii), flat(cp_rr), flat(cp_ii), km.reshape(SSM_NK, g * c, c))


def _gather_chunks(u_ref, length, m):
    xs = [u_ref[pl.ds(s, m, stride=length), :].astype(BF16) for s in range(length)]
    return jnp.concatenate(xs, axis=1)


SSM_SQ = 2 * SSM_SW // LANES
SSM_HQ = SSM_SQ // 2


def _ssm_body(u_ref, km_ref, bpr_ref, bpi_ref, cpr_ref, cpi_ref, are_ref, aim_ref, h0r_ref, h0i_ref,
              d_ref, *rest, length, m, n_seq, n_chunks, in_place):
    if in_place:
        rest = rest[1:]
    o_ref, finr_ref, fini_ref, v_s, ws_s, wct_s, s_s, hp_s = rest
    x = _gather_chunks(u_ref, length, m)

    erow = lax.broadcasted_iota(jnp.int32, (SSM_GROUP, LANES), 0)
    elane = lax.broadcasted_iota(jnp.int32, (SSM_GROUP, LANES), 1)
    spread = jnp.where(elane % SSM_GROUP == erow, 1.0, 0.0).astype(BF16)
    rgrp = lax.broadcasted_iota(jnp.int32, (LANES, LANES), 0) // SSM_GROUP
    lane = lax.broadcasted_iota(jnp.int32, (LANES, LANES), 1)
    for q in range(length):
        blk = jnp.dot(km_ref[length - 1 - q].astype(BF16), spread, preferred_element_type=F32)
        v_s[q * LANES:(q + 1) * LANES, :] = jnp.where(rgrp == lane // SSM_GROUP, blk, 0.0).astype(BF16)
    for s in range(length):
        rows = slice(s * LANES, (s + 1) * LANES)
        for q in range(SSM_HQ):
            own = rgrp == 2 * q + lane // SSM_P
            re_t = slice(q * LANES, (q + 1) * LANES)
            im_t = slice((SSM_HQ + q) * LANES, (SSM_HQ + q + 1) * LANES)
            ws_s[rows, re_t] = jnp.where(own, bpr_ref[length - 1 - s], 0.0).astype(BF16)
            ws_s[rows, im_t] = jnp.where(own, bpi_ref[length - 1 - s], 0.0).astype(BF16)
            wct_s[rows, re_t] = jnp.where(own, cpr_ref[s + 1], 0.0).astype(BF16)
            wct_s[rows, im_t] = jnp.where(own, cpi_ref[s + 1], 0.0).astype(BF16)

    s_loc = jnp.dot(x, ws_s[...], preferred_element_type=F32)
    for q in range(SSM_SQ):
        s_s[q] = s_loc[:, q * LANES:(q + 1) * LANES]

    if m != n_seq * n_chunks:
        hp_s[...] = jnp.zeros(hp_s.shape, F32)
    ar = [are_ref[:, q * LANES:(q + 1) * LANES] for q in range(SSM_HQ)]
    ai = [aim_ref[:, q * LANES:(q + 1) * LANES] for q in range(SSM_HQ)]

    def advance(rows, hs):
        out_r, out_i = [], []
        for q in range(SSM_HQ):
            hr, hi = hs[q], hs[SSM_HQ + q]
            hp_s[q, rows, :] = hr
            hp_s[SSM_HQ + q, rows, :] = hi
            out_r.append(ar[q] * hr - ai[q] * hi + s_s[q, rows, :])
            out_i.append(ar[q] * hi + ai[q] * hr + s_s[SSM_HQ + q, rows, :])
        return tuple(out_r + out_i)

    hs = tuple([h0r_ref[:, q * LANES:(q + 1) * LANES] for q in range(SSM_HQ)]
               + [h0i_ref[:, q * LANES:(q + 1) * LANES] for q in range(SSM_HQ)])
    if n_chunks == 1:
        hs = advance(pl.ds(0, n_seq), hs)
    else:
        hs = lax.fori_loop(0, n_chunks,
                           lambda c, hs: advance(pl.ds(c, n_seq, stride=n_chunks), hs), hs)
    for q in range(SSM_HQ):
        finr_ref[:, q * LANES:(q + 1) * LANES] = hs[q]
        fini_ref[:, q * LANES:(q + 1) * LANES] = hs[SSM_HQ + q]

    hp = jnp.concatenate([hp_s[q].astype(BF16) for q in range(SSM_SQ)], axis=1)
    d = d_ref[...]
    for t in range(length):
        rows = pl.ds(t, m, stride=length)
        y = jnp.dot(x[:, :(t + 1) * LANES], v_s[(length - 1 - t) * LANES:, :],
                    preferred_element_type=F32)
        y = y + lax.dot_general(hp, wct_s[t * LANES:(t + 1) * LANES, :], NT_DIMS,
                                preferred_element_type=F32)
        o_ref[rows, :] = y + d * u_ref[rows, :]


def _ssm_apply(u_all, y_all, ops, a_re, a_im, h0_re, h0_im, dvec, *, length, m, n_seq, n_chunks,
               row_blk):
    in_place = y_all is not None
    bp_rr, bp_ii, cp_rr, cp_ii, km = ops
    rows = pl.BlockSpec((m * length, LANES), lambda j: (row_blk, j))
    comp = lambda last: pl.BlockSpec((SSM_NK, LANES, last), lambda j: (0, j, 0))
    vec = pl.BlockSpec((1, SSM_SW), lambda j: (0, j))
    st = pl.BlockSpec((n_seq, SSM_SW), lambda j: (0, j))
    in_specs = [rows, comp(SSM_GROUP)] + [comp(LANES)] * 4 + [vec, vec, st, st,
                pl.BlockSpec((1, LANES), lambda j: (0, j))]
    args = [u_all, km, bp_rr, bp_ii, cp_rr, cp_ii, a_re, a_im, h0_re, h0_im, dvec]
    if in_place:
        in_specs.append(pl.BlockSpec(memory_space=pl.ANY))
        args.append(y_all)
    return pl.pallas_call(
        functools.partial(_ssm_body, length=length, m=m, n_seq=n_seq, n_chunks=n_chunks,
                          in_place=in_place),
        grid=(SSM_NJ,),
        in_specs=in_specs,
        out_specs=[rows, st, st],
        out_shape=[jax.ShapeDtypeStruct((ROWS, D_MODEL), F32),
                   jax.ShapeDtypeStruct((n_seq, N_SSM_GROUPS * SSM_P), F32),
                   jax.ShapeDtypeStruct((n_seq, N_SSM_GROUPS * SSM_P), F32)],
        input_output_aliases={len(args) - 1: 0} if in_place else {},
        scratch_shapes=[pltpu.VMEM((length * LANES, LANES), BF16),
                        pltpu.VMEM((length * LANES, SSM_SQ * LANES), BF16),
                        pltpu.VMEM((length * LANES, SSM_SQ * LANES), BF16),
                        pltpu.VMEM((SSM_SQ, m, LANES), F32),
                        pltpu.VMEM((SSM_SQ, m, LANES), F32)],
        compiler_params=_cparams(("parallel",)),
        name="ssm_sample" if in_place else "ssm_prompt",
    )(*args)


def kernel(x_prompt, x_sample, cache_k, cache_v, state_conv, state_ssm_re, state_ssm_im, page_table, meta_tokens, norm_mix_pre, norm_mix_post, norm_ffn_pre, norm_ffn_post, w_in_even, lambda_q, lambda_k, subln_g, conv_w, conv_b, conv_ln_g, conv_ln_b, w_out_even, w_in_odd, ssm_a_re, ssm_a_im, ssm_b_re, ssm_b_im, ssm_c_re, ssm_c_im, ssm_d, ssm_log_dt, w_glu, w_out_odd, w_ffn_gate, w_ffn_up, w_ffn_down):
    n_pool = cache_k.shape[1]
    meta = jnp.broadcast_to(meta_tokens.astype(F32)[None], (BATCH, N_META, D_MODEL))
    x = jnp.concatenate([
        jnp.concatenate([meta, x_prompt], axis=1).reshape(ROWS_P, D_MODEL),
        x_sample.reshape(ROWS_S, D_MODEL),
        jnp.zeros((ROWS - ROWS_P - ROWS_S, D_MODEL), F32)], axis=0)
    slopes_vec = jnp.array([2.0 ** (-8.0 * (i + 1) / N_DH) for i in range(N_DH)], dtype=F32)
    slopes = jnp.broadcast_to(slopes_vec[:, None, None], (N_DH, 1, LANES))
    sample = slice(ROWS_P, ROWS_P + ROWS_S)

    ks, vs, convs_p, convs_s, sre_p, sim_p, sre_s, sim_s = [], [], [], [], [], [], [], []
    for layer in range(DEPTH):
        if layer % 2 == 0:
            e = layer // 2
            lam_init = 0.8 - 0.6 * math.exp(-0.3 * (2 * e))
            w_in = w_in_even[e].astype(BF16)
            hb = _rms_cast(x, norm_mix_pre[layer])
            q_all = _mm(hb, [w_in[:, 0:D_ATT]])
            k_all = _mm(hb, [w_in[:, D_ATT:2 * D_ATT]])
            v_all = _mm(hb, [w_in[:, 2 * D_ATT:3 * D_ATT]])
            g_all = _mm(hb, [w_in[:, 3 * D_ATT:3 * D_ATT + D_CONV], w_in[:, 3 * D_ATT + D_CONV:]])
            o_all = _prompt_attn(q_all, k_all, v_all, slopes, lambda_q[e], lambda_k[e], subln_g[e],
                                 lam_init)
            o_s = _decode_attn(page_table,
                               q_all[sample].reshape(DEC_BATCH, DEC_SEQ, D_ATT),
                               k_all[sample].reshape(DEC_BATCH, DEC_SEQ, D_ATT),
                               v_all[sample].reshape(DEC_BATCH, DEC_SEQ, D_ATT),
                               jnp.transpose(cache_k[e], (0, 2, 3, 4, 1)).reshape(n_pool, D_ATT, PAGE_SIZE),
                               cache_v[e].reshape(n_pool, PAGE_SIZE * N_DH, DV),
                               slopes_vec, lambda_q[e], lambda_k[e], subln_g[e], lam_init)
            tail = jnp.zeros((ROWS - ROWS_P - ROWS_S, D_ATT), F32)
            o_all = lax.dynamic_update_slice(
                o_all, jnp.concatenate([o_s.reshape(ROWS_S, D_ATT), tail], axis=0), (ROWS_P, 0))
            g_s = g_all[sample].reshape(DEC_BATCH, DEC_SEQ, D_CONV)
            ext_s = jnp.concatenate([state_conv[e], g_s], axis=1)
            c_all = _conv_prompt(g_all, jnp.zeros((BATCH, CONV_PRE, D_CONV), F32), conv_w[e], conv_b[e])
            c_s = _conv_sample(ext_s, conv_w[e], conv_b[e])
            c_all = lax.dynamic_update_slice(c_all, jnp.concatenate([c_s, tail], axis=0), (ROWS_P, 0))
            w_out = w_out_even[e].astype(BF16)
            x = _even_out(o_all, c_all, conv_ln_g[e], conv_ln_b[e], w_out[:D_ATT], w_out[D_ATT:],
                          norm_mix_post[layer], x)
            ks.append((k_all[:ROWS_P].reshape(BATCH, T_P, N_DH, 2, DK),
                       k_all[sample].reshape(DEC_BATCH, DEC_SEQ, N_DH, 2, DK)))
            vs.append((v_all[:ROWS_P].reshape(BATCH, T_P, N_DH, DV),
                       v_all[sample].reshape(DEC_BATCH, DEC_SEQ, N_DH, DV)))
            convs_p.append(g_all[:ROWS_P].reshape(BATCH, T_P, D_CONV)[:, T_P - (CONV_W - 1):])
            convs_s.append(ext_s[:, DEC_SEQ:])
        else:
            o = layer // 2
            u_all = _norm_mm(x, norm_mix_pre[layer], [w_in_odd[o].astype(BF16)])
            gp = N_SSM_GROUPS * SSM_P
            ap_re, ap_im, *ops = _ssm_params(
                ssm_a_re[o], ssm_a_im[o], ssm_log_dt[o], ssm_b_re[o], ssm_b_im[o],
                ssm_c_re[o], ssm_c_im[o])
            dvec = ssm_d[o].reshape(1, D_MODEL)
            zeros = jnp.zeros((BATCH, gp), F32)
            y_all, pr, pi = _ssm_apply(
                u_all, None, ops, ap_re[SSM_L:SSM_L + 1], ap_im[SSM_L:SSM_L + 1], zeros, zeros, dvec,
                length=SSM_L, m=SSM_M, n_seq=BATCH, n_chunks=SSM_NC, row_blk=0)
            y_all, sr, si = _ssm_apply(
                u_all, y_all, ops, ap_re[DEC_SEQ:DEC_SEQ + 1], ap_im[DEC_SEQ:DEC_SEQ + 1],
                state_ssm_re[o].reshape(DEC_BATCH, gp), state_ssm_im[o].reshape(DEC_BATCH, gp), dvec,
                length=DEC_SEQ, m=DEC_BATCH, n_seq=DEC_BATCH, n_chunks=1, row_blk=SAMPLE_BLK)
            x = _odd_out(y_all, x, w_glu[o].astype(BF16), w_out_odd[o].astype(BF16),
                         norm_mix_post[layer])
            sre_p.append(pr.reshape(BATCH, N_SSM_GROUPS, SSM_P))
            sim_p.append(pi.reshape(BATCH, N_SSM_GROUPS, SSM_P))
            sre_s.append(sr.reshape(DEC_BATCH, N_SSM_GROUPS, SSM_P))
            sim_s.append(si.reshape(DEC_BATCH, N_SSM_GROUPS, SSM_P))
        x = _ffn(x, norm_ffn_pre[layer], w_ffn_gate[layer].astype(BF16), w_ffn_up[layer].astype(BF16),
                 w_ffn_down[layer].astype(BF16), norm_ffn_post[layer])

    y_prompt = x[:ROWS_P].reshape(BATCH, T_P, D_MODEL)[:, N_META:]
    y_sample = x[sample].reshape(DEC_BATCH, DEC_SEQ, D_MODEL)
    return (y_prompt, y_sample,
            jnp.stack([k[0] for k in ks]), jnp.stack([v[0] for v in vs]),
            jnp.stack([k[1] for k in ks]), jnp.stack([v[1] for v in vs]),
            jnp.stack(convs_p), jnp.stack(convs_s),
            jnp.stack(sre_p), jnp.stack(sim_p), jnp.stack(sre_s), jnp.stack(sim_s))
```

```python
import functools
import math

import jax
import jax.numpy as jnp
from jax import lax
from jax.experimental import pallas as pl
from jax.experimental.pallas import tpu as pltpu

F32 = jnp.float32
BF16 = jnp.bfloat16

D_MODEL = 2048
BATCH = 4
SEQ = 2048
DEPTH = 2
DEC_BATCH = 8
DEC_SEQ = 4
PAGE_SIZE = 128
N_META = 16
D_ATT = D_MODEL // 2
N_DH = 8
DK = D_ATT // N_DH // 2
DV = 2 * DK
D_CONV = D_MODEL - D_ATT
CONV_W = 31
SSM_GROUP = 16
N_SSM_GROUPS = D_MODEL // SSM_GROUP
SSM_P = 64
D_FF = ((8 * D_MODEL // 3 + 255) // 256) * 256
EPS = 1e-6

T_P = N_META + SEQ
ROWS_P = BATCH * T_P
ROWS_S = DEC_BATCH * DEC_SEQ
TM = 640
ROWS = 8320
assert ROWS % TM == 0 and ROWS >= ROWS_P + ROWS_S
assert ROWS_P % ROWS_S == 0
SAMPLE_BLK = ROWS_P // ROWS_S

LANES = 128
SUBLANES = 8
VMEM_LIMIT = 56 * 1024 * 1024

SSM_L = 8
SSM_NC = T_P // SSM_L
SSM_M = ROWS // SSM_L
assert SSM_L >= DEC_SEQ
SSM_NJ = D_MODEL // LANES
SSM_G8 = LANES // SSM_GROUP
SSM_SW = SSM_G8 * SSM_P
assert T_P % SSM_L == 0 and ROWS % SSM_L == 0

ATT_T = 256
ATT_NT = SEQ // ATT_T
LOG2E = 1.4426950408889634
NT_DIMS = (((1,), (1,)), ((), ()))
DEC_PP = 8


def _cparams(sem, vmem=VMEM_LIMIT):
    return pltpu.CompilerParams(dimension_semantics=sem, vmem_limit_bytes=vmem)


def _rms(x, g):
    ms = jnp.mean(x * x, axis=-1, keepdims=True)
    return x * lax.rsqrt(ms + EPS) * g


def _rms_cast_body(x_ref, g_ref, o_ref):
    o_ref[...] = _rms(x_ref[...], g_ref[...]).astype(BF16)


def _rms_cast(x, g):
    rows, d = x.shape
    return pl.pallas_call(
        _rms_cast_body,
        grid=(rows // TM,),
        in_specs=[pl.BlockSpec((TM, d), lambda i: (i, 0)), pl.BlockSpec((1, d), lambda i: (0, 0))],
        out_specs=pl.BlockSpec((TM, d), lambda i: (i, 0)),
        out_shape=jax.ShapeDtypeStruct((rows, d), BF16),
        compiler_params=_cparams(("parallel",)),
        name="rms_cast",
    )(x, g.reshape(1, d))


def _mm_body(h_ref, *refs, glu):
    h = h_ref[...]
    a = jnp.dot(h, refs[0][...], preferred_element_type=F32)
    if glu:
        a = a * jax.nn.sigmoid(jnp.dot(h, refs[1][...], preferred_element_type=F32))
    refs[-1][...] = a


def _mm(h, ws, *, tn=512):
    rows, d = h.shape
    n = ws[0].shape[1]
    glu = len(ws) == 2
    return pl.pallas_call(
        functools.partial(_mm_body, glu=glu),
        grid=(rows // TM, n // tn),
        in_specs=[pl.BlockSpec((TM, d), lambda i, j: (i, 0))]
                 + [pl.BlockSpec((d, tn), lambda i, j: (0, j)) for _ in ws],
        out_specs=pl.BlockSpec((TM, tn), lambda i, j: (i, j)),
        out_shape=jax.ShapeDtypeStruct((rows, n), F32),
        compiler_params=_cparams(("parallel", "arbitrary")),
        name="mm_glu" if glu else "mm",
    )(h, *ws)


def _ffn_body(x_ref, gpre_ref, wg_ref, wu_ref, wd_ref, gpost_ref, o_ref, hb, acc):
    j = pl.program_id(1)

    @pl.when(j == 0)
    def _():
        hb[...] = _rms(x_ref[...], gpre_ref[...]).astype(BF16)

    h = hb[...]
    gate = jnp.dot(h, wg_ref[...], preferred_element_type=F32)
    up = jnp.dot(h, wu_ref[...], preferred_element_type=F32)
    a = (jax.nn.silu(gate) * up).astype(BF16)
    part = jnp.dot(a, wd_ref[...], preferred_element_type=F32)

    @pl.when(j == 0)
    def _():
        acc[...] = part

    @pl.when(j > 0)
    def _():
        acc[...] += part

    @pl.when(j == pl.num_programs(1) - 1)
    def _():
        o_ref[...] = x_ref[...] + _rms(acc[...], gpost_ref[...])


def _ffn(x, gpre, wg, wu, wd, gpost, *, tf=512):
    rows, d = x.shape
    dff = wg.shape[1]
    return pl.pallas_call(
        _ffn_body,
        grid=(rows // TM, dff // tf),
        in_specs=[pl.BlockSpec((TM, d), lambda i, j: (i, 0)),
                  pl.BlockSpec((1, d), lambda i, j: (0, 0)),
                  pl.BlockSpec((d, tf), lambda i, j: (0, j)),
                  pl.BlockSpec((d, tf), lambda i, j: (0, j)),
                  pl.BlockSpec((tf, d), lambda i, j: (j, 0)),
                  pl.BlockSpec((1, d), lambda i, j: (0, 0))],
        out_specs=pl.BlockSpec((TM, d), lambda i, j: (i, 0)),
        out_shape=jax.ShapeDtypeStruct((rows, d), F32),
        scratch_shapes=[pltpu.VMEM((TM, d), BF16), pltpu.VMEM((TM, d), F32)],
        compiler_params=_cparams(("parallel", "arbitrary")),
        name="ffn",
    )(x, gpre.reshape(1, d), wg, wu, wd, gpost.reshape(1, d))


def _odd_out_body(y_ref, res_ref, wglu_ref, wout_ref, gpost_ref, o_ref, yf, hb, acc, *, tf):
    j = pl.program_id(1)
    nj = pl.num_programs(1)

    @pl.when(j == 0)
    def _():
        for jj in range(yf.shape[0]):
            gy = jax.nn.gelu(y_ref[:, jj * tf:(jj + 1) * tf])
            yf[jj] = gy
            hb[:, jj * tf:(jj + 1) * tf] = gy.astype(BF16)

    t = jnp.dot(hb[...], wglu_ref[...], preferred_element_type=F32)
    a = (yf[j] * jax.nn.sigmoid(t)).astype(BF16)
    part = jnp.dot(a, wout_ref[...], preferred_element_type=F32)

    @pl.when(j == 0)
    def _():
        acc[...] = part

    @pl.when(j > 0)
    def _():
        acc[...] += part

    @pl.when(j == nj - 1)
    def _():
        o_ref[...] = res_ref[...] + _rms(acc[...], gpost_ref[...])


def _odd_out(y, res, wglu, wout, gpost, *, tf=512, tm=TM):
    rows, d = y.shape
    once = pl.Buffered(1)
    return pl.pallas_call(
        functools.partial(_odd_out_body, tf=tf),
        grid=(rows // tm, d // tf),
        in_specs=[pl.BlockSpec((tm, d), lambda i, j: (i, 0)),
                  pl.BlockSpec((tm, d), lambda i, j: (i, 0), pipeline_mode=once),
                  pl.BlockSpec((d, tf), lambda i, j: (0, j)),
                  pl.BlockSpec((tf, d), lambda i, j: (j, 0)),
                  pl.BlockSpec((1, d), lambda i, j: (0, 0))],
        out_specs=pl.BlockSpec((tm, d), lambda i, j: (i, 0)),
        out_shape=jax.ShapeDtypeStruct((rows, d), F32),
        scratch_shapes=[pltpu.VMEM((d // tf, tm, tf), F32), pltpu.VMEM((tm, d), BF16),
                        pltpu.VMEM((tm, d), F32)],
        compiler_params=_cparams(("parallel", "arbitrary")),
        name="odd_out",
    )(y, res, wglu, wout, gpost.reshape(1, d))


def _even_out_body(o_ref, c_ref, lng_ref, lnb_ref, wtop_ref, wbot_ref, gpost_ref, res_ref, out_ref):
    c = c_ref[...]
    mu = jnp.mean(c, axis=-1, keepdims=True)
    var = jnp.mean(jnp.square(c - mu), axis=-1, keepdims=True)
    cn = jax.nn.silu((c - mu) * lax.rsqrt(var + EPS) * lng_ref[...] + lnb_ref[...])
    y = jnp.dot(o_ref[...].astype(BF16), wtop_ref[...], preferred_element_type=F32)
    y = y + jnp.dot(cn.astype(BF16), wbot_ref[...], preferred_element_type=F32)
    out_ref[...] = res_ref[...] + _rms(y, gpost_ref[...])


def _even_out(o, c, lng, lnb, wtop, wbot, gpost, res):
    rows, d = res.shape
    da, dc = o.shape[1], c.shape[1]
    return pl.pallas_call(
        _even_out_body,
        grid=(rows // TM,),
        in_specs=[pl.BlockSpec((TM, da), lambda i: (i, 0)),
                  pl.BlockSpec((TM, dc), lambda i: (i, 0)),
                  pl.BlockSpec((1, dc), lambda i: (0, 0)),
                  pl.BlockSpec((1, dc), lambda i: (0, 0)),
                  pl.BlockSpec((da, d), lambda i: (0, 0)),
                  pl.BlockSpec((dc, d), lambda i: (0, 0)),
                  pl.BlockSpec((1, d), lambda i: (0, 0)),
                  pl.BlockSpec((TM, d), lambda i: (i, 0))],
        out_specs=pl.BlockSpec((TM, d), lambda i: (i, 0)),
        out_shape=jax.ShapeDtypeStruct((rows, d), F32),
        compiler_params=_cparams(("parallel",)),
        name="even_out",
    )(o, c, lng.reshape(1, dc), lnb.reshape(1, dc), wtop, wbot, gpost.reshape(1, d), res)


def _lam_from(lq_ref, lk_ref, lam_init):
    s0 = jnp.sum(lq_ref[0:1, :] * lk_ref[0:1, :], axis=-1, keepdims=True)
    s1 = jnp.sum(lq_ref[1:2, :] * lk_ref[1:2, :], axis=-1, keepdims=True)
    return jnp.exp(s0) - jnp.exp(s1) + lam_init


def _softmax_step(carry, q2, kc, vc, bias, k_is_transposed=False):
    m, l, acc = carry
    if k_is_transposed:
        s = jnp.dot(q2, kc, preferred_element_type=F32) + bias
    else:
        s = lax.dot_general(q2, kc, (((1,), (1,)), ((), ())), preferred_element_type=F32) + bias
    m_new = jnp.maximum(m, jnp.max(s, axis=-1, keepdims=True))
    alpha = jnp.exp(m - m_new)
    p = jnp.exp(s - m_new)
    l = alpha * l + jnp.sum(p, axis=-1, keepdims=True)
    acc = alpha * acc + jnp.dot(p.astype(BF16), vc, preferred_element_type=F32)
    return m_new, l, acc


def _stack_maps(q, scale):
    lane = lax.broadcasted_iota(jnp.int32, q.shape, 1)
    qs = q * scale
    q0 = jnp.where(lane < DK, qs, 0.0)
    q1 = jnp.where(lane >= DK, qs, 0.0)
    return jnp.concatenate([q0, q1], axis=0).astype(BF16)


def _diff_out(m, l, acc, n, lam, sg, lam_init):
    o = acc[:n] / l[:n] - lam * (acc[n:] / l[n:])
    return _rms(o, sg) * (1.0 - lam_init)


def _prompt_attn_body(slope_ref, lq_ref, lk_ref, sg_ref, q_ref, k_ref, v_ref, o_ref,
                      kb, vb, nb_s, *, lam_init):
    slope = slope_ref[0, 0:1, 0:1]
    lam = _lam_from(lq_ref, lk_ref, lam_init)
    sg = sg_ref[...]
    scale = DK ** -0.5
    kb[...] = k_ref[...].astype(BF16)
    vb[:, 0:DV] = v_ref[...].astype(BF16)
    neg_inf = float("-inf")

    zpad = jnp.zeros((LANES - N_META, LANES), BF16)
    k_meta = jnp.concatenate([kb[0:N_META, :], zpad], axis=0)
    v_meta = jnp.concatenate([vb[0:N_META, 0:DV], zpad], axis=0)

    rm = lax.broadcasted_iota(jnp.int32, (2 * N_META, LANES), 0) % N_META
    cm = lax.broadcasted_iota(jnp.int32, (2 * N_META, LANES), 1)
    bias_m = jnp.where(cm <= rm, -slope * (rm - cm).astype(F32), neg_inf)
    q2 = _stack_maps(q_ref[0:N_META, :], scale)
    init = (jnp.full((2 * N_META, 1), neg_inf, F32), jnp.zeros((2 * N_META, 1), F32),
            jnp.zeros((2 * N_META, LANES), F32))
    m, l, acc = _softmax_step(init, q2, k_meta, v_meta, bias_m)
    o_ref[0:N_META, :] = _diff_out(m, l, acc, N_META, lam, sg, lam_init)

    sl2 = slope * LOG2E
    vb[:, DV:2 * DV] = jnp.ones((T_P, DV), BF16)
    rr = lax.broadcasted_iota(jnp.int32, (2 * ATT_T, ATT_T), 0) % ATT_T
    cc = lax.broadcasted_iota(jnp.int32, (2 * ATT_T, ATT_T), 1)
    nbase = -sl2 * (rr - cc).astype(F32)
    nb_s[0] = nbase
    nb_s[1] = jnp.where(cc < N_META, nbase, neg_inf)
    nb_s[2] = jnp.where(cc <= rr, nbase, neg_inf)

    for i in range(ATT_NT):
        q0 = N_META + i * ATT_T
        q2 = _stack_maps(q_ref[q0:q0 + ATT_T, :], scale * LOG2E)
        tiles = [(0, 1, float(N_META + i * ATT_T))]
        tiles += [(N_META + j * ATT_T, 0, float((i - j) * ATT_T)) for j in range(i)]
        tiles += [(q0, 2, 0.0)]
        scores, mx = [], jnp.full((2 * ATT_T, LANES), neg_inf, F32)
        for k0, kind, off in tiles:
            s = lax.dot_general(q2, kb[k0:k0 + ATT_T, :], NT_DIMS, preferred_element_type=F32)
            s = s + nb_s[kind]
            mx = jnp.maximum(mx, jnp.maximum(s[:, 0:LANES], s[:, LANES:2 * LANES]) - sl2 * off)
            scores.append(s)
        m = jnp.max(mx, axis=-1, keepdims=True)
        probs = []
        for (k0, kind, off), s in zip(tiles, scores):
            probs.append(jnp.exp2(s - (m + sl2 * off)).astype(BF16))
        p_all = jnp.concatenate(probs, axis=1)
        v_all = jnp.concatenate([vb[0:ATT_T, :], vb[N_META:q0 + ATT_T, :]], axis=0)
        acc = jnp.dot(p_all, v_all, preferred_element_type=F32)
        o0 = acc[0:ATT_T, 0:DV] / acc[0:ATT_T, DV:2 * DV]
        o1 = acc[ATT_T:2 * ATT_T, 0:DV] / acc[ATT_T:2 * ATT_T, DV:2 * DV]
        o_ref[q0:q0 + ATT_T, :] = _rms(o0 - lam * o1, sg) * (1.0 - lam_init)


def _prompt_attn(q_all, k_all, v_all, slopes, lq, lk, sg, lam_init):
    blk = pl.BlockSpec((T_P, DV), lambda b, h: (b, h))
    return pl.pallas_call(
        functools.partial(_prompt_attn_body, lam_init=lam_init),
        grid=(BATCH, N_DH),
        in_specs=[pl.BlockSpec((1, 1, LANES), lambda b, h: (h, 0, 0)),
                  pl.BlockSpec((2, DK), lambda b, h: (0, 0)),
                  pl.BlockSpec((2, DK), lambda b, h: (0, 0)),
                  pl.BlockSpec((1, DV), lambda b, h: (0, 0)),
                  blk, blk, blk],
        out_specs=blk,
        out_shape=jax.ShapeDtypeStruct((ROWS, D_ATT), F32),
        scratch_shapes=[pltpu.VMEM((T_P, DV), BF16),
                        pltpu.VMEM((T_P, 2 * DV), BF16),
                        pltpu.VMEM((3, 2 * ATT_T, ATT_T), F32)],
        compiler_params=_cparams(("parallel", "parallel")),
        name="prompt_attn",
    )(slopes, lq, lk, sg.reshape(1, DV), q_all, k_all, v_all)


def _decode_attn_body(pt_ref, srow_ref, qi_ref, lq_ref, lk_ref, sg_ref, q_ref, kn_ref, vn_ref, *refs,
                      lam_init, n_pages):
    k_refs, v_refs = refs[:DEC_PP], refs[DEC_PP:2 * DEC_PP]
    o_ref, m_s, l_s, acc_s = refs[2 * DEC_PP:]
    s_id = pl.program_id(1)
    n_rows = 2 * N_DH * DEC_SEQ
    past_len = n_pages * PAGE_SIZE
    neg_inf = float("-inf")

    row = lax.broadcasted_iota(jnp.int32, (n_rows, D_ATT), 0)
    lane = lax.broadcasted_iota(jnp.int32, (n_rows, D_ATT), 1)
    q2 = jnp.where(lane // DK == row // DEC_SEQ, q_ref[0] * (DK ** -0.5), 0.0).astype(BF16)
    srow = srow_ref[...]
    qi = qi_ref[...]
    col = lax.broadcasted_iota(jnp.int32, (n_rows, PAGE_SIZE), 1).astype(F32)
    srow_w = jnp.concatenate([srow] * DEC_PP, axis=1)
    qi_w = jnp.concatenate([qi] * DEC_PP, axis=1)
    col_w = lax.broadcasted_iota(jnp.int32, (n_rows, DEC_PP * PAGE_SIZE), 1).astype(F32)

    @pl.when(s_id == 0)
    def _():
        m_s[...] = jnp.full(m_s.shape, neg_inf, F32)
        l_s[...] = jnp.zeros(l_s.shape, F32)
        acc_s[...] = jnp.zeros(acc_s.shape, F32)

    def step(kc, vc, bias, k_is_transposed=False):
        m, l, acc = _softmax_step((m_s[:, 0:1], l_s[:, 0:1], acc_s[...]), q2, kc, vc, bias,
                                  k_is_transposed)
        m_s[...] = jnp.broadcast_to(m, m_s.shape)
        l_s[...] = jnp.broadcast_to(l, l_s.shape)
        acc_s[...] = acc

    kt = jnp.concatenate([k_refs[p][0].astype(BF16) for p in range(DEC_PP)], axis=1)
    vc = jnp.concatenate(
        [jnp.concatenate(
            [v_refs[p][0, pl.ds(h, PAGE_SIZE, stride=N_DH), :].astype(BF16) for h in range(N_DH)],
            axis=1) for p in range(DEC_PP)], axis=0)
    kpos0 = jnp.asarray(s_id * (DEC_PP * PAGE_SIZE), F32)
    dist = (past_len + qi_w) - (kpos0 + col_w)
    step(kt, vc, -srow_w * dist, k_is_transposed=True)

    @pl.when(s_id == pl.num_programs(1) - 1)
    def _():
        dist = qi - col
        bias = jnp.where(dist >= 0, -srow * dist, neg_inf)
        step(kn_ref[0].astype(BF16), vn_ref[0].astype(BF16), bias)
        lam = _lam_from(lq_ref, lk_ref, lam_init)
        sg = sg_ref[...]
        l = l_s[:, 0:1]
        for h in range(N_DH):
            blk = acc_s[h * 2 * DEC_SEQ:(h + 1) * 2 * DEC_SEQ, h * DV:(h + 1) * DV]
            blk = blk / l[h * 2 * DEC_SEQ:(h + 1) * 2 * DEC_SEQ]
            o = blk[0:DEC_SEQ] - lam * blk[DEC_SEQ:2 * DEC_SEQ]
            o_ref[0, :, h * DV:(h + 1) * DV] = _rms(o, sg) * (1.0 - lam_init)


def _decode_attn(page_table, q_s, k_s, v_s, cache_k, cache_v, slopes_vec, lq, lk, sg, lam_init):
    n_pages = page_table.shape[1]
    n_rows = 2 * N_DH * DEC_SEQ
    q_t = jnp.tile(q_s, (1, 2 * N_DH, 1))
    pad = ((0, 0), (0, PAGE_SIZE - DEC_SEQ), (0, 0))
    kn = jnp.pad(k_s, pad)
    vn = jnp.pad(v_s, pad)
    ridx = jnp.arange(n_rows)
    srow = jnp.broadcast_to(slopes_vec[ridx // (2 * DEC_SEQ)][:, None], (n_rows, PAGE_SIZE)).astype(F32)
    qi = jnp.broadcast_to((ridx % DEC_SEQ)[:, None], (n_rows, PAGE_SIZE)).astype(F32)

    def page_spec(p, shape):
        return pl.BlockSpec((1,) + shape, lambda b, s, pt: (pt[b, s * DEC_PP + p], 0, 0))

    const2 = lambda b, s, pt: (0, 0)
    per_b = lambda b, s, pt: (b, 0, 0)
    grid_spec = pltpu.PrefetchScalarGridSpec(
        num_scalar_prefetch=1,
        grid=(DEC_BATCH, n_pages // DEC_PP),
        in_specs=[pl.BlockSpec((n_rows, PAGE_SIZE), const2),
                  pl.BlockSpec((n_rows, PAGE_SIZE), const2),
                  pl.BlockSpec((2, DK), const2),
                  pl.BlockSpec((2, DK), const2),
                  pl.BlockSpec((1, DV), const2),
                  pl.BlockSpec((1, n_rows, D_ATT), per_b),
                  pl.BlockSpec((1, PAGE_SIZE, D_ATT), per_b),
                  pl.BlockSpec((1, PAGE_SIZE, D_ATT), per_b)]
                 + [page_spec(p, (D_ATT, PAGE_SIZE)) for p in range(DEC_PP)]
                 + [page_spec(p, (PAGE_SIZE * N_DH, DV)) for p in range(DEC_PP)],
        out_specs=pl.BlockSpec((1, DEC_SEQ, D_ATT), per_b),
        scratch_shapes=[pltpu.VMEM((n_rows, LANES), F32), pltpu.VMEM((n_rows, LANES), F32),
                        pltpu.VMEM((n_rows, D_ATT), F32)],
    )
    return pl.pallas_call(
        functools.partial(_decode_attn_body, lam_init=lam_init, n_pages=n_pages),
        grid_spec=grid_spec,
        out_shape=jax.ShapeDtypeStruct((DEC_BATCH, DEC_SEQ, D_ATT), F32),
        compiler_params=_cparams(("parallel", "arbitrary")),
        name="decode_attn",
    )(page_table, srow, qi, lq, lk, sg.reshape(1, DV), q_t, kn, vn,
      *([cache_k] * DEC_PP), *([cache_v] * DEC_PP))


CONV_PRE = 32
CONV_RC = 48
CONV_CT = 256
assert T_P % CONV_RC == 0


def _conv_prompt_body(prev_ref, g_ref, w_ref, b_ref, o_ref, gp, wr_s):
    gp[0:CONV_PRE, :] = prev_ref[0]
    gp[CONV_PRE:, :] = g_ref[...]
    lead = CONV_PRE - (CONV_W - 1)
    win_rows = CONV_RC + CONV_PRE
    bias = b_ref[...]

    def chunk(ci, _):
        t0 = pl.multiple_of(ci * CONV_RC, SUBLANES)
        win = gp[pl.ds(t0, win_rows), :]
        acc = jnp.broadcast_to(bias, (CONV_RC, CONV_CT))
        for r in range(SUBLANES):
            taps = [k for k in range(CONV_W) if (k + lead) % SUBLANES == r]
            if not taps:
                continue
            hi = max(k + lead for k in taps) - r + CONV_RC
            if r:
                wr_s[0:hi, :] = win[r:r + hi, :]
            for k in taps:
                a = k + lead - r
                src = wr_s[a:a + CONV_RC, :] if r else win[a:a + CONV_RC, :]
                acc = acc + w_ref[k:k + 1, :] * src
        o_ref[pl.ds(t0, CONV_RC), :] = acc
        return 0

    lax.fori_loop(0, T_P // CONV_RC, chunk, 0)


def _conv_prompt(g_all, prev, w, b):
    return pl.pallas_call(
        _conv_prompt_body,
        grid=(BATCH, D_CONV // CONV_CT),
        in_specs=[pl.BlockSpec((1, CONV_PRE, CONV_CT), lambda bi, j: (bi, 0, j)),
                  pl.BlockSpec((T_P, CONV_CT), lambda bi, j: (bi, j)),
                  pl.BlockSpec((CONV_W, CONV_CT), lambda bi, j: (0, j)),
                  pl.BlockSpec((1, CONV_CT), lambda bi, j: (0, j))],
        out_specs=pl.BlockSpec((T_P, CONV_CT), lambda bi, j: (bi, j)),
        out_shape=jax.ShapeDtypeStruct((ROWS, D_CONV), F32),
        scratch_shapes=[pltpu.VMEM((CONV_PRE + T_P, CONV_CT), F32),
                        pltpu.VMEM((CONV_RC + CONV_PRE, CONV_CT), F32)],
        compiler_params=_cparams(("parallel", "parallel")),
        name="conv_prompt",
    )(prev, g_all, w, b.reshape(1, D_CONV))


def _conv_sample_body(ext_ref, w_ref, b_ref, o_ref):
    w = w_ref[...]
    for bi in range(DEC_BATCH):
        for t in range(DEC_SEQ):
            acc = jnp.sum(w * ext_ref[bi, t:t + CONV_W, :], axis=0, keepdims=True) + b_ref[...]
            o_ref[bi * DEC_SEQ + t:bi * DEC_SEQ + t + 1, :] = acc


def _conv_sample(ext, w, b):
    return pl.pallas_call(
        _conv_sample_body,
        out_shape=jax.ShapeDtypeStruct((ROWS_S, D_CONV), F32),
        name="conv_sample",
    )(ext, w, b.reshape(1, D_CONV))


SSM_NK = SSM_L + 1


def _ssm_param_body(lre_ref, lim_ref, ldt_ref, bre_ref, bim_ref, cre_ref, cim_ref,
                    ap_re, ap_im, bp_rr, bp_ii, cp_rr, cp_ii, km_ref,
                    ar_s, ai_s, pr_s, pi_s, bbr_s, bbi_s):
    k = pl.program_id(0)

    @pl.when(k == 0)
    def _():
        lr, li = lre_ref[...], lim_ref[...]
        dt = jnp.exp(ldt_ref[...])
        er = jnp.exp(lr * dt)
        ar = er * jnp.cos(li * dt)
        ai = er * jnp.sin(li * dt)
        den = lr * lr + li * li
        xr, xi = ar - 1.0, ai
        cr = (xr * lr + xi * li) / den
        ci = (xi * lr - xr * li) / den
        br, bi = bre_ref[...], bim_ref[...]
        bbr_s[...] = cr * br - ci * bi
        bbi_s[...] = cr * bi + ci * br
        ar_s[...] = ar
        ai_s[...] = ai
        pr_s[...] = jnp.ones(pr_s.shape, F32)
        pi_s[...] = jnp.zeros(pi_s.shape, F32)

    @pl.when(k > 0)
    def _():
        pr, pi = pr_s[...], pi_s[...]
        ar, ai = ar_s[...], ai_s[...]
        pr_s[...] = pr * ar - pi * ai
        pi_s[...] = pr * ai + pi * ar

    pr, pi = pr_s[...], pi_s[...]
    ap_re[0] = pr
    ap_im[0] = pi
    bbr, bbi = bbr_s[...], bbi_s[...]
    bpr = pr * bbr - pi * bbi
    bpi = pr * bbi + pi * bbr
    bp_rr[0] = jnp.concatenate([bpr, bpr], axis=-1)
    bp_ii[0] = jnp.concatenate([bpi, bpi], axis=-1)
    cr, ci = cre_ref[...], cim_ref[...]
    cpr = pr * cr - pi * ci
    cpi = -(pr * ci + pi * cr)
    cp_rr[0] = jnp.concatenate([cpr, cpr], axis=-1)
    cp_ii[0] = jnp.concatenate([cpi, cpi], axis=-1)
    dn = (((2,), (2,)), ((0,), (0,)))
    hp = lax.Precision.HIGHEST
    km_ref[0] = (lax.dot_general(bpr, cr, dn, precision=hp, preferred_element_type=F32)
                 - lax.dot_general(bpi, ci, dn, precision=hp, preferred_element_type=F32))


def _ssm_params(lre, lim, log_dt, b_re, b_im, c_re, c_im):
    g, p, c = N_SSM_GROUPS, SSM_P, SSM_GROUP
    bc = lambda a: jnp.broadcast_to(a[:, None, :], (g, c, p))
    ldt = jnp.broadcast_to(log_dt[:, None, None], (g, c, p))
    bt_re = jnp.swapaxes(b_re, 1, 2)
    bt_im = jnp.swapaxes(b_im, 1, 2)
    gcp = pl.BlockSpec((g, c, p), lambda k: (0, 0, 0))
    o_gcp = pl.BlockSpec((1, g, c, p), lambda k: (k, 0, 0, 0))
    o_wide = pl.BlockSpec((1, g, c, 2 * p), lambda k: (k, 0, 0, 0))
    narrow = jax.ShapeDtypeStruct((SSM_NK, g, c, p), F32)
    wide = jax.ShapeDtypeStruct((SSM_NK, g, c, 2 * p), F32)
    ap_re, ap_im, bp_rr, bp_ii, cp_rr, cp_ii, km = pl.pallas_call(
        _ssm_param_body,
        grid=(SSM_NK,),
        in_specs=[gcp] * 7,
        out_specs=[o_gcp] * 2 + [o_wide] * 4 + [pl.BlockSpec((1, g, c, c), lambda k: (k, 0, 0, 0))],
        out_shape=[narrow] * 2 + [wide] * 4 + [jax.ShapeDtypeStruct((SSM_NK, g, c, c), F32)],
        scratch_shapes=[pltpu.VMEM((g, c, p), F32)] * 6,
        compiler_params=_cparams(("arbitrary",)),
        name="ssm_params",
    )(bc(lre), bc(lim), ldt, bt_re, bt_im, c_re, c_im)
    flat = lambda a: a.reshape(SSM_NK, g * c, 2 * p)
    return (ap_re[:, :, 0, :].reshape(SSM_NK, g * p), ap_im[:, :, 0, :].reshape(SSM_NK, g * p),
            flat(bp_rr), flat(bp_ii), flat(cp_rr), flat(cp_ii), km.reshape(SSM_NK, g * c, c))


def _gather_chunks(u_ref, length, m):
    xs = [u_ref[pl.ds(s, m, stride=length), :].astype(BF16) for s in range(length)]
    return jnp.concatenate(xs, axis=1)


SSM_SQ = 2 * SSM_SW // LANES
SSM_HQ = SSM_SQ // 2


def _ssm_body(u_ref, km_ref, bpr_ref, bpi_ref, cpr_ref, cpi_ref, are_ref, aim_ref, h0r_ref, h0i_ref,
              d_ref, *rest, length, m, n_seq, n_chunks, in_place):
    if in_place:
        rest = rest[1:]
    o_ref, finr_ref, fini_ref, v_s, ws_s, wct_s, s_s, hp_s = rest
    x = _gather_chunks(u_ref, length, m)

    erow = lax.broadcasted_iota(jnp.int32, (SSM_GROUP, LANES), 0)
    elane = lax.broadcasted_iota(jnp.int32, (SSM_GROUP, LANES), 1)
    spread = jnp.where(elane % SSM_GROUP == erow, 1.0, 0.0).astype(BF16)
    rgrp = lax.broadcasted_iota(jnp.int32, (LANES, LANES), 0) // SSM_GROUP
    lane = lax.broadcasted_iota(jnp.int32, (LANES, LANES), 1)
    for q in range(length):
        blk = jnp.dot(km_ref[length - 1 - q].astype(BF16), spread, preferred_element_type=F32)
        v_s[q * LANES:(q + 1) * LANES, :] = jnp.where(rgrp == lane // SSM_GROUP, blk, 0.0).astype(BF16)
    for s in range(length):
        rows = slice(s * LANES, (s + 1) * LANES)
        for q in range(SSM_HQ):
            own = rgrp == 2 * q + lane // SSM_P
            re_t = slice(q * LANES, (q + 1) * LANES)
            im_t = slice((SSM_HQ + q) * LANES, (SSM_HQ + q + 1) * LANES)
            ws_s[rows, re_t] = jnp.where(own, bpr_ref[length - 1 - s], 0.0).astype(BF16)
            ws_s[rows, im_t] = jnp.where(own, bpi_ref[length - 1 - s], 0.0).astype(BF16)
            wct_s[rows, re_t] = jnp.where(own, cpr_ref[s + 1], 0.0).astype(BF16)
            wct_s[rows, im_t] = jnp.where(own, cpi_ref[s + 1], 0.0).astype(BF16)

    s_loc = jnp.dot(x, ws_s[...], preferred_element_type=F32)
    for q in range(SSM_SQ):
        s_s[q] = s_loc[:, q * LANES:(q + 1) * LANES]

    if m != n_seq * n_chunks:
        hp_s[...] = jnp.zeros(hp_s.shape, F32)
    ar = [are_ref[:, q * LANES:(q + 1) * LANES] for q in range(SSM_HQ)]
    ai = [aim_ref[:, q * LANES:(q + 1) * LANES] for q in range(SSM_HQ)]

    def advance(rows, hs):
        out_r, out_i = [], []
        for q in range(SSM_HQ):
            hr, hi = hs[q], hs[SSM_HQ + q]
            hp_s[q, rows, :] = hr
            hp_s[SSM_HQ + q, rows, :] = hi
            out_r.append(ar[q] * hr - ai[q] * hi + s_s[q, rows, :])
            out_i.append(ar[q] * hi + ai[q] * hr + s_s[SSM_HQ + q, rows, :])
        return tuple(out_r + out_i)

    hs = tuple([h0r_ref[:, q * LANES:(q + 1) * LANES] for q in range(SSM_HQ)]
               + [h0i_ref[:, q * LANES:(q + 1) * LANES] for q in range(SSM_HQ)])
    if n_chunks == 1:
        hs = advance(pl.ds(0, n_seq), hs)
    else:
        hs = lax.fori_loop(0, n_chunks,
                           lambda c, hs: advance(pl.ds(c, n_seq, stride=n_chunks), hs), hs)
    for q in range(SSM_HQ):
        finr_ref[:, q * LANES:(q + 1) * LANES] = hs[q]
        fini_ref[:, q * LANES:(q + 1) * LANES] = hs[SSM_HQ + q]

    hp = jnp.concatenate([hp_s[q].astype(BF16) for q in range(SSM_SQ)], axis=1)
    d = d_ref[...]
    for t in range(length):
        rows = pl.ds(t, m, stride=length)
        y = jnp.dot(x[:, :(t + 1) * LANES], v_s[(length - 1 - t) * LANES:, :],
                    preferred_element_type=F32)
        y = y + lax.dot_general(hp, wct_s[t * LANES:(t + 1) * LANES, :], NT_DIMS,
                                preferred_element_type=F32)
        o_ref[rows, :] = y + d * u_ref[rows, :]


def _ssm_apply(u_all, y_all, ops, a_re, a_im, h0_re, h0_im, dvec, *, length, m, n_seq, n_chunks,
               row_blk):
    in_place = y_all is not None
    bp_rr, bp_ii, cp_rr, cp_ii, km = ops
    rows = pl.BlockSpec((m * length, LANES), lambda j: (row_blk, j))
    comp = lambda last: pl.BlockSpec((SSM_NK, LANES, last), lambda j: (0, j, 0))
    vec = pl.BlockSpec((1, SSM_SW), lambda j: (0, j))
    st = pl.BlockSpec((n_seq, SSM_SW), lambda j: (0, j))
    in_specs = [rows, comp(SSM_GROUP)] + [comp(LANES)] * 4 + [vec, vec, st, st,
                pl.BlockSpec((1, LANES), lambda j: (0, j))]
    args = [u_all, km, bp_rr, bp_ii, cp_rr, cp_ii, a_re, a_im, h0_re, h0_im, dvec]
    if in_place:
        in_specs.append(pl.BlockSpec(memory_space=pl.ANY))
        args.append(y_all)
    return pl.pallas_call(
        functools.partial(_ssm_body, length=length, m=m, n_seq=n_seq, n_chunks=n_chunks,
                          in_place=in_place),
        grid=(SSM_NJ,),
        in_specs=in_specs,
        out_specs=[rows, st, st],
        out_shape=[jax.ShapeDtypeStruct((ROWS, D_MODEL), F32),
                   jax.ShapeDtypeStruct((n_seq, N_SSM_GROUPS * SSM_P), F32),
                   jax.ShapeDtypeStruct((n_seq, N_SSM_GROUPS * SSM_P), F32)],
        input_output_aliases={len(args) - 1: 0} if in_place else {},
        scratch_shapes=[pltpu.VMEM((length * LANES, LANES), BF16),
                        pltpu.VMEM((length * LANES, SSM_SQ * LANES), BF16),
                        pltpu.VMEM((length * LANES, SSM_SQ * LANES), BF16),
                        pltpu.VMEM((SSM_SQ, m, LANES), F32),
                        pltpu.VMEM((SSM_SQ, m, LANES), F32)],
        compiler_params=_cparams(("parallel",)),
        name="ssm_sample" if in_place else "ssm_prompt",
    )(*args)


def kernel(x_prompt, x_sample, cache_k, cache_v, state_conv, state_ssm_re, state_ssm_im, page_table, meta_tokens, norm_mix_pre, norm_mix_post, norm_ffn_pre, norm_ffn_post, w_in_even, lambda_q, lambda_k, subln_g, conv_w, conv_b, conv_ln_g, conv_ln_b, w_out_even, w_in_odd, ssm_a_re, ssm_a_im, ssm_b_re, ssm_b_im, ssm_c_re, ssm_c_im, ssm_d, ssm_log_dt, w_glu, w_out_odd, w_ffn_gate, w_ffn_up, w_ffn_down):
    n_pool = cache_k.shape[1]
    meta = jnp.broadcast_to(meta_tokens.astype(F32)[None], (BATCH, N_META, D_MODEL))
    x = jnp.concatenate([
        jnp.concatenate([meta, x_prompt], axis=1).reshape(ROWS_P, D_MODEL),
        x_sample.reshape(ROWS_S, D_MODEL),
        jnp.zeros((ROWS - ROWS_P - ROWS_S, D_MODEL), F32)], axis=0)
    slopes_vec = jnp.array([2.0 ** (-8.0 * (i + 1) / N_DH) for i in range(N_DH)], dtype=F32)
    slopes = jnp.broadcast_to(slopes_vec[:, None, None], (N_DH, 1, LANES))
    sample = slice(ROWS_P, ROWS_P + ROWS_S)

    ks, vs, convs_p, convs_s, sre_p, sim_p, sre_s, sim_s = [], [], [], [], [], [], [], []
    for layer in range(DEPTH):
        if layer % 2 == 0:
            e = layer // 2
            lam_init = 0.8 - 0.6 * math.exp(-0.3 * (2 * e))
            w_in = w_in_even[e].astype(BF16)
            hb = _rms_cast(x, norm_mix_pre[layer])
            q_all = _mm(hb, [w_in[:, 0:D_ATT]])
            k_all = _mm(hb, [w_in[:, D_ATT:2 * D_ATT]])
            v_all = _mm(hb, [w_in[:, 2 * D_ATT:3 * D_ATT]])
            g_all = _mm(hb, [w_in[:, 3 * D_ATT:3 * D_ATT + D_CONV], w_in[:, 3 * D_ATT + D_CONV:]])
            o_all = _prompt_attn(q_all, k_all, v_all, slopes, lambda_q[e], lambda_k[e], subln_g[e],
                                 lam_init)
            o_s = _decode_attn(page_table,
                               q_all[sample].reshape(DEC_BATCH, DEC_SEQ, D_ATT),
                               k_all[sample].reshape(DEC_BATCH, DEC_SEQ, D_ATT),
                               v_all[sample].reshape(DEC_BATCH, DEC_SEQ, D_ATT),
                               jnp.transpose(cache_k[e], (0, 2, 3, 4, 1)).reshape(n_pool, D_ATT, PAGE_SIZE),
                               cache_v[e].reshape(n_pool, PAGE_SIZE * N_DH, DV),
                               slopes_vec, lambda_q[e], lambda_k[e], subln_g[e], lam_init)
            tail = jnp.zeros((ROWS - ROWS_P - ROWS_S, D_ATT), F32)
            o_all = lax.dynamic_update_slice(
                o_all, jnp.concatenate([o_s.reshape(ROWS_S, D_ATT), tail], axis=0), (ROWS_P, 0))
            g_s = g_all[sample].reshape(DEC_BATCH, DEC_SEQ, D_CONV)
            ext_s = jnp.concatenate([state_conv[e], g_s], axis=1)
            c_all = _conv_prompt(g_all, jnp.zeros((BATCH, CONV_PRE, D_CONV), F32), conv_w[e], conv_b[e])
            c_s = _conv_sample(ext_s, conv_w[e], conv_b[e])
            c_all = lax.dynamic_update_slice(c_all, jnp.concatenate([c_s, tail], axis=0), (ROWS_P, 0))
            w_out = w_out_even[e].astype(BF16)
            x = _even_out(o_all, c_all, conv_ln_g[e], conv_ln_b[e], w_out[:D_ATT], w_out[D_ATT:],
                          norm_mix_post[layer], x)
            ks.append((k_all[:ROWS_P].reshape(BATCH, T_P, N_DH, 2, DK),
                       k_all[sample].reshape(DEC_BATCH, DEC_SEQ, N_DH, 2, DK)))
            vs.append((v_all[:ROWS_P].reshape(BATCH, T_P, N_DH, DV),
                       v_all[sample].reshape(DEC_BATCH, DEC_SEQ, N_DH, DV)))
            convs_p.append(g_all[:ROWS_P].reshape(BATCH, T_P, D_CONV)[:, T_P - (CONV_W - 1):])
            convs_s.append(ext_s[:, DEC_SEQ:])
        else:
            o = layer // 2
            u_all = _mm(_rms_cast(x, norm_mix_pre[layer]), [w_in_odd[o].astype(BF16)])
            gp = N_SSM_GROUPS * SSM_P
            ap_re, ap_im, *ops = _ssm_params(
                ssm_a_re[o], ssm_a_im[o], ssm_log_dt[o], ssm_b_re[o], ssm_b_im[o],
                ssm_c_re[o], ssm_c_im[o])
            dvec = ssm_d[o].reshape(1, D_MODEL)
            zeros = jnp.zeros((BATCH, gp), F32)
            y_all, pr, pi = _ssm_apply(
                u_all, None, ops, ap_re[SSM_L:SSM_L + 1], ap_im[SSM_L:SSM_L + 1], zeros, zeros, dvec,
                length=SSM_L, m=SSM_M, n_seq=BATCH, n_chunks=SSM_NC, row_blk=0)
            y_all, sr, si = _ssm_apply(
                u_all, y_all, ops, ap_re[DEC_SEQ:DEC_SEQ + 1], ap_im[DEC_SEQ:DEC_SEQ + 1],
                state_ssm_re[o].reshape(DEC_BATCH, gp), state_ssm_im[o].reshape(DEC_BATCH, gp), dvec,
                length=DEC_SEQ, m=DEC_BATCH, n_seq=DEC_BATCH, n_chunks=1, row_blk=SAMPLE_BLK)
            x = _odd_out(y_all, x, w_glu[o].astype(BF16), w_out_odd[o].astype(BF16),
                         norm_mix_post[layer])
            sre_p.append(pr.reshape(BATCH, N_SSM_GROUPS, SSM_P))
            sim_p.append(pi.reshape(BATCH, N_SSM_GROUPS, SSM_P))
            sre_s.append(sr.reshape(DEC_BATCH, N_SSM_GROUPS, SSM_P))
            sim_s.append(si.reshape(DEC_BATCH, N_SSM_GROUPS, SSM_P))
        x = _ffn(x, norm_ffn_pre[layer], w_ffn_gate[layer].astype(BF16), w_ffn_up[layer].astype(BF16),
                 w_ffn_down[layer].astype(BF16), norm_ffn_post[layer])

    y_prompt = x[:ROWS_P].reshape(BATCH, T_P, D_MODEL)[:, N_META:]
    y_sample = x[sample].reshape(DEC_BATCH, DEC_SEQ, D_MODEL)
    return (y_prompt, y_sample,
            jnp.stack([k[0] for k in ks]), jnp.stack([v[0] for v in vs]),
            jnp.stack([k[1] for k in ks]), jnp.stack([v[1] for v in vs]),
            jnp.stack(convs_p), jnp.stack(convs_s),
            jnp.stack(sre_p), jnp.stack(sim_p), jnp.stack(sre_s), jnp.stack(sim_s))
```

```python
import functools
import math

import jax
import jax.numpy as jnp
from jax import lax
from jax.experimental import pallas as pl
from jax.experimental.pallas import tpu as pltpu

F32 = jnp.float32
BF16 = jnp.bfloat16

D_MODEL = 2048
BATCH = 4
SEQ = 2048
DEPTH = 2
DEC_BATCH = 8
DEC_SEQ = 4
PAGE_SIZE = 128
N_META = 16
D_ATT = D_MODEL // 2
N_DH = 8
DK = D_ATT // N_DH // 2
DV = 2 * DK
D_CONV = D_MODEL - D_ATT
CONV_W = 31
SSM_GROUP = 16
N_SSM_GROUPS = D_MODEL // SSM_GROUP
SSM_P = 64
D_FF = ((8 * D_MODEL // 3 + 255) // 256) * 256
EPS = 1e-6

T_P = N_META + SEQ
ROWS_P = BATCH * T_P
ROWS_S = DEC_BATCH * DEC_SEQ
TM = 640
ROWS = 8320
assert ROWS % TM == 0 and ROWS >= ROWS_P + ROWS_S
assert ROWS_P % ROWS_S == 0
SAMPLE_BLK = ROWS_P // ROWS_S

LANES = 128
SUBLANES = 8
VMEM_LIMIT = 56 * 1024 * 1024

SSM_L = 8
SSM_NC = T_P // SSM_L
SSM_M = ROWS // SSM_L
assert SSM_L >= DEC_SEQ
SSM_NJ = D_MODEL // LANES
SSM_G8 = LANES // SSM_GROUP
SSM_SW = SSM_G8 * SSM_P
assert T_P % SSM_L == 0 and ROWS % SSM_L == 0

ATT_T = 256
ATT_NT = SEQ // ATT_T
LOG2E = 1.4426950408889634
NT_DIMS = (((1,), (1,)), ((), ()))
DEC_PP = 8


def _cparams(sem, vmem=VMEM_LIMIT):
    return pltpu.CompilerParams(dimension_semantics=sem, vmem_limit_bytes=vmem)


def _rms(x, g):
    ms = jnp.mean(x * x, axis=-1, keepdims=True)
    return x * lax.rsqrt(ms + EPS) * g


def _rms_cast_body(x_ref, g_ref, o_ref):
    o_ref[...] = _rms(x_ref[...], g_ref[...]).astype(BF16)


def _rms_cast(x, g):
    rows, d = x.shape
    return pl.pallas_call(
        _rms_cast_body,
        grid=(rows // TM,),
        in_specs=[pl.BlockSpec((TM, d), lambda i: (i, 0)), pl.BlockSpec((1, d), lambda i: (0, 0))],
        out_specs=pl.BlockSpec((TM, d), lambda i: (i, 0)),
        out_shape=jax.ShapeDtypeStruct((rows, d), BF16),
        compiler_params=_cparams(("parallel",)),
        name="rms_cast",
    )(x, g.reshape(1, d))


ROWS_T = ROWS - ROWS_P
TAIL_AT = ROWS_P % TM
assert TAIL_AT + ROWS_T == TM


def _mm_body(h_ref, *refs, glu, split):
    n_w = 2 if glu else 1
    h = h_ref[...]
    a = jnp.dot(h, refs[0][...], preferred_element_type=F32)
    if glu:
        a = a * jax.nn.sigmoid(jnp.dot(h, refs[1][...], preferred_element_type=F32))
    refs[n_w][...] = a
    if split:
        refs[n_w + 1][...] = a[TAIL_AT:TM, :]


def _mm(h, ws, *, tn=1024, split=False):
    rows, d = h.shape
    n = ws[0].shape[1]
    glu = len(ws) == 2
    out_specs = [pl.BlockSpec((TM, tn), lambda i, j: (i, j))]
    out_shape = [jax.ShapeDtypeStruct((ROWS_P if split else rows, n), F32)]
    if split:
        out_specs.append(pl.BlockSpec((ROWS_T, tn), lambda i, j: (0, j)))
        out_shape.append(jax.ShapeDtypeStruct((ROWS_T, n), F32))
    outs = pl.pallas_call(
        functools.partial(_mm_body, glu=glu, split=split),
        grid=(rows // TM, n // tn),
        in_specs=[pl.BlockSpec((TM, d), lambda i, j: (i, 0))]
                 + [pl.BlockSpec((d, tn), lambda i, j: (0, j)) for _ in ws],
        out_specs=out_specs,
        out_shape=out_shape,
        compiler_params=_cparams(("arbitrary", "arbitrary")),
        name="mm_glu" if glu else "mm",
    )(h, *ws)
    return outs if split else outs[0]


def _ffn_body(x_ref, gpre_ref, wg_ref, wu_ref, wd_ref, gpost_ref, o_ref, hb, acc):
    j = pl.program_id(1)

    @pl.when(j == 0)
    def _():
        hb[...] = _rms(x_ref[...], gpre_ref[...]).astype(BF16)

    h = hb[...]
    gate = jnp.dot(h, wg_ref[...], preferred_element_type=F32)
    up = jnp.dot(h, wu_ref[...], preferred_element_type=F32)
    a = (jax.nn.silu(gate) * up).astype(BF16)
    part = jnp.dot(a, wd_ref[...], preferred_element_type=F32)

    @pl.when(j == 0)
    def _():
        acc[...] = part

    @pl.when(j > 0)
    def _():
        acc[...] += part

    @pl.when(j == pl.num_programs(1) - 1)
    def _():
        o_ref[...] = x_ref[...] + _rms(acc[...], gpost_ref[...])


def _ffn(x, gpre, wg, wu, wd, gpost, *, tf=512):
    rows, d = x.shape
    dff = wg.shape[1]
    return pl.pallas_call(
        _ffn_body,
        grid=(rows // TM, dff // tf),
        in_specs=[pl.BlockSpec((TM, d), lambda i, j: (i, 0)),
                  pl.BlockSpec((1, d), lambda i, j: (0, 0)),
                  pl.BlockSpec((d, tf), lambda i, j: (0, j)),
                  pl.BlockSpec((d, tf), lambda i, j: (0, j)),
                  pl.BlockSpec((tf, d), lambda i, j: (j, 0)),
                  pl.BlockSpec((1, d), lambda i, j: (0, 0))],
        out_specs=pl.BlockSpec((TM, d), lambda i, j: (i, 0)),
        out_shape=jax.ShapeDtypeStruct((rows, d), F32),
        scratch_shapes=[pltpu.VMEM((TM, d), BF16), pltpu.VMEM((TM, d), F32)],
        compiler_params=_cparams(("parallel", "arbitrary")),
        name="ffn",
    )(x, gpre.reshape(1, d), wg, wu, wd, gpost.reshape(1, d))


def _odd_out_body(y_ref, res_ref, wglu_ref, wout_ref, gpost_ref, o_ref, yf, hb, acc, *, tf):
    j = pl.program_id(1)
    nj = pl.num_programs(1)

    @pl.when(j == 0)
    def _():
        for jj in range(yf.shape[0]):
            gy = jax.nn.gelu(y_ref[:, jj * tf:(jj + 1) * tf])
            yf[jj] = gy
            hb[:, jj * tf:(jj + 1) * tf] = gy.astype(BF16)

    t = jnp.dot(hb[...], wglu_ref[...], preferred_element_type=F32)
    a = (yf[j] * jax.nn.sigmoid(t)).astype(BF16)
    part = jnp.dot(a, wout_ref[...], preferred_element_type=F32)

    @pl.when(j == 0)
    def _():
        acc[...] = part

    @pl.when(j > 0)
    def _():
        acc[...] += part

    @pl.when(j == nj - 1)
    def _():
        o_ref[...] = res_ref[...] + _rms(acc[...], gpost_ref[...])


def _odd_out(y, res, wglu, wout, gpost, *, tf=512, tm=TM):
    rows, d = y.shape
    once = pl.Buffered(1)
    return pl.pallas_call(
        functools.partial(_odd_out_body, tf=tf),
        grid=(rows // tm, d // tf),
        in_specs=[pl.BlockSpec((tm, d), lambda i, j: (i, 0)),
                  pl.BlockSpec((tm, d), lambda i, j: (i, 0), pipeline_mode=once),
                  pl.BlockSpec((d, tf), lambda i, j: (0, j)),
                  pl.BlockSpec((tf, d), lambda i, j: (j, 0)),
                  pl.BlockSpec((1, d), lambda i, j: (0, 0))],
        out_specs=pl.BlockSpec((tm, d), lambda i, j: (i, 0)),
        out_shape=jax.ShapeDtypeStruct((rows, d), F32),
        scratch_shapes=[pltpu.VMEM((d // tf, tm, tf), F32), pltpu.VMEM((tm, d), BF16),
                        pltpu.VMEM((tm, d), F32)],
        compiler_params=_cparams(("parallel", "arbitrary")),
        name="odd_out",
    )(y, res, wglu, wout, gpost.reshape(1, d))


def _even_out_body(o_ref, c_ref, lng_ref, lnb_ref, wtop_ref, wbot_ref, gpost_ref, res_ref, out_ref):
    c = c_ref[...]
    mu = jnp.mean(c, axis=-1, keepdims=True)
    var = jnp.mean(jnp.square(c - mu), axis=-1, keepdims=True)
    cn = jax.nn.silu((c - mu) * lax.rsqrt(var + EPS) * lng_ref[...] + lnb_ref[...])
    y = jnp.dot(o_ref[...].astype(BF16), wtop_ref[...], preferred_element_type=F32)
    y = y + jnp.dot(cn.astype(BF16), wbot_ref[...], preferred_element_type=F32)
    out_ref[...] = res_ref[...] + _rms(y, gpost_ref[...])


def _even_out(o, c, lng, lnb, wtop, wbot, gpost, res):
    rows, d = res.shape
    da, dc = o.shape[1], c.shape[1]
    return pl.pallas_call(
        _even_out_body,
        grid=(rows // TM,),
        in_specs=[pl.BlockSpec((TM, da), lambda i: (i, 0)),
                  pl.BlockSpec((TM, dc), lambda i: (i, 0)),
                  pl.BlockSpec((1, dc), lambda i: (0, 0)),
                  pl.BlockSpec((1, dc), lambda i: (0, 0)),
                  pl.BlockSpec((da, d), lambda i: (0, 0)),
                  pl.BlockSpec((dc, d), lambda i: (0, 0)),
                  pl.BlockSpec((1, d), lambda i: (0, 0)),
                  pl.BlockSpec((TM, d), lambda i: (i, 0))],
        out_specs=pl.BlockSpec((TM, d), lambda i: (i, 0)),
        out_shape=jax.ShapeDtypeStruct((rows, d), F32),
        compiler_params=_cparams(("parallel",)),
        name="even_out",
    )(o, c, lng.reshape(1, dc), lnb.reshape(1, dc), wtop, wbot, gpost.reshape(1, d), res)


def _lam_from(lq_ref, lk_ref, lam_init):
    s0 = jnp.sum(lq_ref[0:1, :] * lk_ref[0:1, :], axis=-1, keepdims=True)
    s1 = jnp.sum(lq_ref[1:2, :] * lk_ref[1:2, :], axis=-1, keepdims=True)
    return jnp.exp(s0) - jnp.exp(s1) + lam_init


def _softmax_step(carry, q2, kc, vc, bias, k_is_transposed=False):
    m, l, acc = carry
    if k_is_transposed:
        s = jnp.dot(q2, kc, preferred_element_type=F32) + bias
    else:
        s = lax.dot_general(q2, kc, (((1,), (1,)), ((), ())), preferred_element_type=F32) + bias
    m_new = jnp.maximum(m, jnp.max(s, axis=-1, keepdims=True))
    alpha = jnp.exp(m - m_new)
    p = jnp.exp(s - m_new)
    l = alpha * l + jnp.sum(p, axis=-1, keepdims=True)
    acc = alpha * acc + jnp.dot(p.astype(BF16), vc, preferred_element_type=F32)
    return m_new, l, acc


def _stack_maps(q, scale):
    lane = lax.broadcasted_iota(jnp.int32, q.shape, 1)
    qs = q * scale
    q0 = jnp.where(lane < DK, qs, 0.0)
    q1 = jnp.where(lane >= DK, qs, 0.0)
    return jnp.concatenate([q0, q1], axis=0).astype(BF16)


def _diff_out(m, l, acc, n, lam, sg, lam_init):
    o = acc[:n] / l[:n] - lam * (acc[n:] / l[n:])
    return _rms(o, sg) * (1.0 - lam_init)


def _prompt_attn_body(slope_ref, lq_ref, lk_ref, sg_ref, q_ref, k_ref, v_ref, o_ref,
                      kb, vb, nb_s, *, lam_init):
    slope = slope_ref[0, 0:1, 0:1]
    lam = _lam_from(lq_ref, lk_ref, lam_init)
    sg = sg_ref[...]
    scale = DK ** -0.5
    kb[...] = k_ref[...].astype(BF16)
    vb[:, 0:DV] = v_ref[...].astype(BF16)
    neg_inf = float("-inf")

    zpad = jnp.zeros((LANES - N_META, LANES), BF16)
    k_meta = jnp.concatenate([kb[0:N_META, :], zpad], axis=0)
    v_meta = jnp.concatenate([vb[0:N_META, 0:DV], zpad], axis=0)

    rm = lax.broadcasted_iota(jnp.int32, (2 * N_META, LANES), 0) % N_META
    cm = lax.broadcasted_iota(jnp.int32, (2 * N_META, LANES), 1)
    bias_m = jnp.where(cm <= rm, -slope * (rm - cm).astype(F32), neg_inf)
    q2 = _stack_maps(q_ref[0:N_META, :], scale)
    init = (jnp.full((2 * N_META, 1), neg_inf, F32), jnp.zeros((2 * N_META, 1), F32),
            jnp.zeros((2 * N_META, LANES), F32))
    m, l, acc = _softmax_step(init, q2, k_meta, v_meta, bias_m)
    o_ref[0:N_META, :] = _diff_out(m, l, acc, N_META, lam, sg, lam_init)

    sl2 = slope * LOG2E
    vb[:, DV:2 * DV] = jnp.ones((T_P, DV), BF16)
    rr = lax.broadcasted_iota(jnp.int32, (2 * ATT_T, ATT_T), 0) % ATT_T
    cc = lax.broadcasted_iota(jnp.int32, (2 * ATT_T, ATT_T), 1)
    nbase = -sl2 * (rr - cc).astype(F32)
    nb_s[0] = nbase
    nb_s[1] = jnp.where(cc <= rr, nbase, neg_inf)
    nb_meta = jnp.where(cc < N_META, nbase, neg_inf)

    for i in range(ATT_NT):
        q0 = N_META + i * ATT_T
        q2 = _stack_maps(q_ref[q0:q0 + ATT_T, :], scale * LOG2E)
        tiles = [(0, None, float(N_META + i * ATT_T))]
        tiles += [(N_META + j * ATT_T, 0, float((i - j) * ATT_T)) for j in range(i)]
        tiles += [(q0, 1, 0.0)]
        scores, mx = [], jnp.full((2 * ATT_T, LANES), neg_inf, F32)
        for k0, kind, off in tiles:
            s = lax.dot_general(q2, kb[k0:k0 + ATT_T, :], NT_DIMS, preferred_element_type=F32)
            s = s + (nb_meta if kind is None else nb_s[kind])
            mx = jnp.maximum(mx, jnp.maximum(s[:, 0:LANES], s[:, LANES:2 * LANES]) - sl2 * off)
            scores.append(s)
        m = jnp.max(mx, axis=-1, keepdims=True)
        probs = []
        for (k0, kind, off), s in zip(tiles, scores):
            probs.append(jnp.exp2(s - (m + sl2 * off)).astype(BF16))
        p_all = jnp.concatenate(probs, axis=1)
        v_all = jnp.concatenate([vb[0:ATT_T, :], vb[N_META:q0 + ATT_T, :]], axis=0)
        acc = jnp.dot(p_all, v_all, preferred_element_type=F32)
        o0 = acc[0:ATT_T, 0:DV] / acc[0:ATT_T, DV:2 * DV]
        o1 = acc[ATT_T:2 * ATT_T, 0:DV] / acc[ATT_T:2 * ATT_T, DV:2 * DV]
        o_ref[q0:q0 + ATT_T, :] = _rms(o0 - lam * o1, sg) * (1.0 - lam_init)


def _prompt_attn(q_all, k_all, v_all, slopes, lq, lk, sg, lam_init):
    blk = pl.BlockSpec((T_P, DV), lambda b, h: (b, h))
    return pl.pallas_call(
        functools.partial(_prompt_attn_body, lam_init=lam_init),
        grid=(BATCH, N_DH),
        in_specs=[pl.BlockSpec((1, 1, LANES), lambda b, h: (h, 0, 0)),
                  pl.BlockSpec((2, DK), lambda b, h: (0, 0)),
                  pl.BlockSpec((2, DK), lambda b, h: (0, 0)),
                  pl.BlockSpec((1, DV), lambda b, h: (0, 0)),
                  blk, blk, blk],
        out_specs=blk,
        out_shape=jax.ShapeDtypeStruct((ROWS_P, D_ATT), F32),
        scratch_shapes=[pltpu.VMEM((T_P, DV), BF16),
                        pltpu.VMEM((T_P, 2 * DV), BF16),
                        pltpu.VMEM((2, 2 * ATT_T, ATT_T), F32)],
        compiler_params=_cparams(("parallel", "parallel")),
        name="prompt_attn",
    )(slopes, lq, lk, sg.reshape(1, DV), q_all, k_all, v_all)


def _decode_attn_body(pt_ref, srow_ref, qi_ref, lq_ref, lk_ref, sg_ref, q_ref, kn_ref, vn_ref, *refs,
                      lam_init, n_pages):
    k_refs, v_refs = refs[:DEC_PP], refs[DEC_PP:2 * DEC_PP]
    o_ref, m_s, l_s, acc_s = refs[2 * DEC_PP:]
    s_id = pl.program_id(1)
    n_rows = 2 * N_DH * DEC_SEQ
    past_len = n_pages * PAGE_SIZE
    neg_inf = float("-inf")

    row = lax.broadcasted_iota(jnp.int32, (n_rows, D_ATT), 0)
    lane = lax.broadcasted_iota(jnp.int32, (n_rows, D_ATT), 1)
    q2 = jnp.where(lane // DK == row // DEC_SEQ, q_ref[0] * (DK ** -0.5), 0.0).astype(BF16)
    srow = srow_ref[...]
    qi = qi_ref[...]
    col = lax.broadcasted_iota(jnp.int32, (n_rows, PAGE_SIZE), 1).astype(F32)
    srow_w = jnp.concatenate([srow] * DEC_PP, axis=1)
    qi_w = jnp.concatenate([qi] * DEC_PP, axis=1)
    col_w = lax.broadcasted_iota(jnp.int32, (n_rows, DEC_PP * PAGE_SIZE), 1).astype(F32)

    @pl.when(s_id == 0)
    def _():
        m_s[...] = jnp.full(m_s.shape, neg_inf, F32)
        l_s[...] = jnp.zeros(l_s.shape, F32)
        acc_s[...] = jnp.zeros(acc_s.shape, F32)

    def step(kc, vc, bias, k_is_transposed=False):
        m, l, acc = _softmax_step((m_s[:, 0:1], l_s[:, 0:1], acc_s[...]), q2, kc, vc, bias,
                                  k_is_transposed)
        m_s[...] = jnp.broadcast_to(m, m_s.shape)
        l_s[...] = jnp.broadcast_to(l, l_s.shape)
        acc_s[...] = acc

    kt = jnp.concatenate([k_refs[p][0].astype(BF16) for p in range(DEC_PP)], axis=1)
    vc = jnp.concatenate(
        [jnp.concatenate(
            [v_refs[p][0, pl.ds(h, PAGE_SIZE, stride=N_DH), :].astype(BF16) for h in range(N_DH)],
            axis=1) for p in range(DEC_PP)], axis=0)
    kpos0 = jnp.asarray(s_id * (DEC_PP * PAGE_SIZE), F32)
    dist = (past_len + qi_w) - (kpos0 + col_w)
    step(kt, vc, -srow_w * dist, k_is_transposed=True)

    @pl.when(s_id == pl.num_programs(1) - 1)
    def _():
        dist = qi - col
        bias = jnp.where(dist >= 0, -srow * dist, neg_inf)
        step(kn_ref[0].astype(BF16), vn_ref[0].astype(BF16), bias)
        lam = _lam_from(lq_ref, lk_ref, lam_init)
        sg = sg_ref[...]
        l = l_s[:, 0:1]
        for h in range(N_DH):
            blk = acc_s[h * 2 * DEC_SEQ:(h + 1) * 2 * DEC_SEQ, h * DV:(h + 1) * DV]
            blk = blk / l[h * 2 * DEC_SEQ:(h + 1) * 2 * DEC_SEQ]
            o = blk[0:DEC_SEQ] - lam * blk[DEC_SEQ:2 * DEC_SEQ]
            o_ref[0, :, h * DV:(h + 1) * DV] = _rms(o, sg) * (1.0 - lam_init)


def _decode_attn(page_table, q_s, k_s, v_s, cache_k, cache_v, slopes_vec, lq, lk, sg, lam_init):
    n_pages = page_table.shape[1]
    n_rows = 2 * N_DH * DEC_SEQ
    q_t = jnp.tile(q_s, (1, 2 * N_DH, 1))
    pad = ((0, 0), (0, PAGE_SIZE - DEC_SEQ), (0, 0))
    kn = jnp.pad(k_s, pad)
    vn = jnp.pad(v_s, pad)
    ridx = jnp.arange(n_rows)
    srow = jnp.broadcast_to(slopes_vec[ridx // (2 * DEC_SEQ)][:, None], (n_rows, PAGE_SIZE)).astype(F32)
    qi = jnp.broadcast_to((ridx % DEC_SEQ)[:, None], (n_rows, PAGE_SIZE)).astype(F32)

    def page_spec(p, shape):
        return pl.BlockSpec((1,) + shape, lambda b, s, pt: (pt[b, s * DEC_PP + p], 0, 0))

    const2 = lambda b, s, pt: (0, 0)
    per_b = lambda b, s, pt: (b, 0, 0)
    grid_spec = pltpu.PrefetchScalarGridSpec(
        num_scalar_prefetch=1,
        grid=(DEC_BATCH, n_pages // DEC_PP),
        in_specs=[pl.BlockSpec((n_rows, PAGE_SIZE), const2),
                  pl.BlockSpec((n_rows, PAGE_SIZE), const2),
                  pl.BlockSpec((2, DK), const2),
                  pl.BlockSpec((2, DK), const2),
                  pl.BlockSpec((1, DV), const2),
                  pl.BlockSpec((1, n_rows, D_ATT), per_b),
                  pl.BlockSpec((1, PAGE_SIZE, D_ATT), per_b),
                  pl.BlockSpec((1, PAGE_SIZE, D_ATT), per_b)]
                 + [page_spec(p, (D_ATT, PAGE_SIZE)) for p in range(DEC_PP)]
                 + [page_spec(p, (PAGE_SIZE * N_DH, DV)) for p in range(DEC_PP)],
        out_specs=pl.BlockSpec((1, DEC_SEQ, D_ATT), per_b),
        scratch_shapes=[pltpu.VMEM((n_rows, LANES), F32), pltpu.VMEM((n_rows, LANES), F32),
                        pltpu.VMEM((n_rows, D_ATT), F32)],
    )
    return pl.pallas_call(
        functools.partial(_decode_attn_body, lam_init=lam_init, n_pages=n_pages),
        grid_spec=grid_spec,
        out_shape=jax.ShapeDtypeStruct((DEC_BATCH, DEC_SEQ, D_ATT), F32),
        compiler_params=_cparams(("parallel", "arbitrary")),
        name="decode_attn",
    )(page_table, srow, qi, lq, lk, sg.reshape(1, DV), q_t, kn, vn,
      *([cache_k] * DEC_PP), *([cache_v] * DEC_PP))


CONV_PRE = 32
CONV_RC = 48
CONV_CT = 256
assert T_P % CONV_RC == 0


def _conv_prompt_body(prev_ref, g_ref, w_ref, b_ref, o_ref, gp, wr_s):
    gp[0:CONV_PRE, :] = prev_ref[0]
    gp[CONV_PRE:, :] = g_ref[...]
    lead = CONV_PRE - (CONV_W - 1)
    win_rows = CONV_RC + CONV_PRE
    bias = b_ref[...]

    def chunk(ci, _):
        t0 = pl.multiple_of(ci * CONV_RC, SUBLANES)
        win = gp[pl.ds(t0, win_rows), :]
        acc = jnp.broadcast_to(bias, (CONV_RC, CONV_CT))
        for r in range(SUBLANES):
            taps = [k for k in range(CONV_W) if (k + lead) % SUBLANES == r]
            if not taps:
                continue
            hi = max(k + lead for k in taps) - r + CONV_RC
            if r:
                wr_s[0:hi, :] = win[r:r + hi, :]
            for k in taps:
                a = k + lead - r
                src = wr_s[a:a + CONV_RC, :] if r else win[a:a + CONV_RC, :]
                acc = acc + w_ref[k:k + 1, :] * src
        o_ref[pl.ds(t0, CONV_RC), :] = acc
        return 0

    lax.fori_loop(0, T_P // CONV_RC, chunk, 0)


def _conv_prompt(g_all, prev, w, b):
    return pl.pallas_call(
        _conv_prompt_body,
        grid=(BATCH, D_CONV // CONV_CT),
        in_specs=[pl.BlockSpec((1, CONV_PRE, CONV_CT), lambda bi, j: (bi, 0, j)),
                  pl.BlockSpec((T_P, CONV_CT), lambda bi, j: (bi, j)),
                  pl.BlockSpec((CONV_W, CONV_CT), lambda bi, j: (0, j)),
                  pl.BlockSpec((1, CONV_CT), lambda bi, j: (0, j))],
        out_specs=pl.BlockSpec((T_P, CONV_CT), lambda bi, j: (bi, j)),
        out_shape=jax.ShapeDtypeStruct((ROWS_P, D_CONV), F32),
        scratch_shapes=[pltpu.VMEM((CONV_PRE + T_P, CONV_CT), F32),
                        pltpu.VMEM((CONV_RC + CONV_PRE, CONV_CT), F32)],
        compiler_params=_cparams(("parallel", "parallel")),
        name="conv_prompt",
    )(prev, g_all, w, b.reshape(1, D_CONV))


def _conv_sample_body(ext_ref, w_ref, b_ref, o_ref):
    w = w_ref[...]
    for bi in range(DEC_BATCH):
        for t in range(DEC_SEQ):
            acc = jnp.sum(w * ext_ref[bi, t:t + CONV_W, :], axis=0, keepdims=True) + b_ref[...]
            o_ref[bi * DEC_SEQ + t:bi * DEC_SEQ + t + 1, :] = acc


def _conv_sample(ext, w, b):
    return pl.pallas_call(
        _conv_sample_body,
        out_shape=jax.ShapeDtypeStruct((ROWS_S, D_CONV), F32),
        name="conv_sample",
    )(ext, w, b.reshape(1, D_CONV))


SSM_NK = SSM_L + 1


def _ssm_param_body(lre_ref, lim_ref, ldt_ref, bre_ref, bim_ref, cre_ref, cim_ref,
                    ap_re, ap_im, bp_rr, bp_ii, cp_rr, cp_ii, km_ref,
                    ar_s, ai_s, pr_s, pi_s, bbr_s, bbi_s):
    k = pl.program_id(0)

    @pl.when(k == 0)
    def _():
        lr, li = lre_ref[...], lim_ref[...]
        dt = jnp.exp(ldt_ref[...])
        er = jnp.exp(lr * dt)
        ar = er * jnp.cos(li * dt)
        ai = er * jnp.sin(li * dt)
        den = lr * lr + li * li
        xr, xi = ar - 1.0, ai
        cr = (xr * lr + xi * li) / den
        ci = (xi * lr - xr * li) / den
        br, bi = bre_ref[...], bim_ref[...]
        bbr_s[...] = cr * br - ci * bi
        bbi_s[...] = cr * bi + ci * br
        ar_s[...] = ar
        ai_s[...] = ai
        pr_s[...] = jnp.ones(pr_s.shape, F32)
        pi_s[...] = jnp.zeros(pi_s.shape, F32)

    @pl.when(k > 0)
    def _():
        pr, pi = pr_s[...], pi_s[...]
        ar, ai = ar_s[...], ai_s[...]
        pr_s[...] = pr * ar - pi * ai
        pi_s[...] = pr * ai + pi * ar

    pr, pi = pr_s[...], pi_s[...]
    ap_re[0] = pr
    ap_im[0] = pi
    bbr, bbi = bbr_s[...], bbi_s[...]
    bpr = pr * bbr - pi * bbi
    bpi = pr * bbi + pi * bbr
    bp_rr[0] = jnp.concatenate([bpr, bpr], axis=-1)
    bp_ii[0] = jnp.concatenate([bpi, bpi], axis=-1)
    cr, ci = cre_ref[...], cim_ref[...]
    cpr = pr * cr - pi * ci
    cpi = -(pr * ci + pi * cr)
    cp_rr[0] = jnp.concatenate([cpr, cpr], axis=-1)
    cp_ii[0] = jnp.concatenate([cpi, cpi], axis=-1)
    dn = (((2,), (2,)), ((0,), (0,)))
    hp = lax.Precision.HIGHEST
    km_ref[0] = (lax.dot_general(bpr, cr, dn, precision=hp, preferred_element_type=F32)
                 - lax.dot_general(bpi, ci, dn, precision=hp, preferred_element_type=F32))


def _ssm_params(lre, lim, log_dt, b_re, b_im, c_re, c_im):
    g, p, c = N_SSM_GROUPS, SSM_P, SSM_GROUP
    bc = lambda a: jnp.broadcast_to(a[:, None, :], (g, c, p))
    ldt = jnp.broadcast_to(log_dt[:, None, None], (g, c, p))
    bt_re = jnp.swapaxes(b_re, 1, 2)
    bt_im = jnp.swapaxes(b_im, 1, 2)
    gcp = pl.BlockSpec((g, c, p), lambda k: (0, 0, 0))
    o_gcp = pl.BlockSpec((1, g, c, p), lambda k: (k, 0, 0, 0))
    o_wide = pl.BlockSpec((1, g, c, 2 * p), lambda k: (k, 0, 0, 0))
    narrow = jax.ShapeDtypeStruct((SSM_NK, g, c, p), F32)
    wide = jax.ShapeDtypeStruct((SSM_NK, g, c, 2 * p), F32)
    ap_re, ap_im, bp_rr, bp_ii, cp_rr, cp_ii, km = pl.pallas_call(
        _ssm_param_body,
        grid=(SSM_NK,),
        in_specs=[gcp] * 7,
        out_specs=[o_gcp] * 2 + [o_wide] * 4 + [pl.BlockSpec((1, g, c, c), lambda k: (k, 0, 0, 0))],
        out_shape=[narrow] * 2 + [wide] * 4 + [jax.ShapeDtypeStruct((SSM_NK, g, c, c), F32)],
        scratch_shapes=[pltpu.VMEM((g, c, p), F32)] * 6,
        compiler_params=_cparams(("arbitrary",)),
        name="ssm_params",
    )(bc(lre), bc(lim), ldt, bt_re, bt_im, c_re, c_im)
    flat = lambda a: a.reshape(SSM_NK, g * c, 2 * p)
    return (ap_re[:, :, 0, :].reshape(SSM_NK, g * p), ap_im[:, :, 0, :].reshape(SSM_NK, g * p),
            flat(bp_rr), flat(bp_ii), flat(cp_rr), flat(cp_ii), km.reshape(SSM_NK, g * c, c))


def _gather_chunks(u_ref, length, m):
    xs = [u_ref[pl.ds(s, m, stride=length), :].astype(BF16) for s in range(length)]
    return jnp.concatenate(xs, axis=1)


SSM_SQ = 2 * SSM_SW // LANES
SSM_HQ = SSM_SQ // 2


def _ssm_body(u_ref, km_ref, bpr_ref, bpi_ref, cpr_ref, cpi_ref, are_ref, aim_ref, h0r_ref, h0i_ref,
              d_ref, *rest, length, m, n_seq, n_chunks, in_place):
    if in_place:
        rest = rest[1:]
    o_ref, finr_ref, fini_ref, v_s, ws_s, wct_s, s_s, hp_s = rest
    x = _gather_chunks(u_ref, length, m)

    erow = lax.broadcasted_iota(jnp.int32, (SSM_GROUP, LANES), 0)
    elane = lax.broadcasted_iota(jnp.int32, (SSM_GROUP, LANES), 1)
    spread = jnp.where(elane % SSM_GROUP == erow, 1.0, 0.0).astype(BF16)
    rgrp = lax.broadcasted_iota(jnp.int32, (LANES, LANES), 0) // SSM_GROUP
    lane = lax.broadcasted_iota(jnp.int32, (LANES, LANES), 1)
    for q in range(length):
        blk = jnp.dot(km_ref[length - 1 - q].astype(BF16), spread, preferred_element_type=F32)
        v_s[q * LANES:(q + 1) * LANES, :] = jnp.where(rgrp == lane // SSM_GROUP, blk, 0.0).astype(BF16)
    for s in range(length):
        rows = slice(s * LANES, (s + 1) * LANES)
        for q in range(SSM_HQ):
            own = rgrp == 2 * q + lane // SSM_P
            re_t = slice(q * LANES, (q + 1) * LANES)
            im_t = slice((SSM_HQ + q) * LANES, (SSM_HQ + q + 1) * LANES)
            ws_s[rows, re_t] = jnp.where(own, bpr_ref[length - 1 - s], 0.0).astype(BF16)
            ws_s[rows, im_t] = jnp.where(own, bpi_ref[length - 1 - s], 0.0).astype(BF16)
            wct_s[rows, re_t] = jnp.where(own, cpr_ref[s + 1], 0.0).astype(BF16)
            wct_s[rows, im_t] = jnp.where(own, cpi_ref[s + 1], 0.0).astype(BF16)

    s_loc = jnp.dot(x, ws_s[...], preferred_element_type=F32)
    for q in range(SSM_SQ):
        s_s[q] = s_loc[:, q * LANES:(q + 1) * LANES]

    if m != n_seq * n_chunks:
        hp_s[...] = jnp.zeros(hp_s.shape, F32)
    ar = [are_ref[:, q * LANES:(q + 1) * LANES] for q in range(SSM_HQ)]
    ai = [aim_ref[:, q * LANES:(q + 1) * LANES] for q in range(SSM_HQ)]

    def advance(rows, hs):
        out_r, out_i = [], []
        for q in range(SSM_HQ):
            hr, hi = hs[q], hs[SSM_HQ + q]
            hp_s[q, rows, :] = hr
            hp_s[SSM_HQ + q, rows, :] = hi
            out_r.append(ar[q] * hr - ai[q] * hi + s_s[q, rows, :])
            out_i.append(ar[q] * hi + ai[q] * hr + s_s[SSM_HQ + q, rows, :])
        return tuple(out_r + out_i)

    hs = tuple([h0r_ref[:, q * LANES:(q + 1) * LANES] for q in range(SSM_HQ)]
               + [h0i_ref[:, q * LANES:(q + 1) * LANES] for q in range(SSM_HQ)])
    if n_chunks == 1:
        hs = advance(pl.ds(0, n_seq), hs)
    else:
        hs = lax.fori_loop(0, n_chunks,
                           lambda c, hs: advance(pl.ds(c, n_seq, stride=n_chunks), hs), hs)
    for q in range(SSM_HQ):
        finr_ref[:, q * LANES:(q + 1) * LANES] = hs[q]
        fini_ref[:, q * LANES:(q + 1) * LANES] = hs[SSM_HQ + q]

    hp = jnp.concatenate([hp_s[q].astype(BF16) for q in range(SSM_SQ)], axis=1)
    d = d_ref[...]
    for t in range(length):
        rows = pl.ds(t, m, stride=length)
        y = jnp.dot(x[:, :(t + 1) * LANES], v_s[(length - 1 - t) * LANES:, :],
                    preferred_element_type=F32)
        y = y + lax.dot_general(hp, wct_s[t * LANES:(t + 1) * LANES, :], NT_DIMS,
                                preferred_element_type=F32)
        o_ref[rows, :] = y + d * u_ref[rows, :]


def _ssm_apply(u_all, y_all, ops, a_re, a_im, h0_re, h0_im, dvec, *, length, m, n_seq, n_chunks,
               row_blk):
    in_place = y_all is not None
    bp_rr, bp_ii, cp_rr, cp_ii, km = ops
    rows = pl.BlockSpec((m * length, LANES), lambda j: (row_blk, j))
    comp = lambda last: pl.BlockSpec((SSM_NK, LANES, last), lambda j: (0, j, 0))
    vec = pl.BlockSpec((1, SSM_SW), lambda j: (0, j))
    st = pl.BlockSpec((n_seq, SSM_SW), lambda j: (0, j))
    in_specs = [rows, comp(SSM_GROUP)] + [comp(LANES)] * 4 + [vec, vec, st, st,
                pl.BlockSpec((1, LANES), lambda j: (0, j))]
    args = [u_all, km, bp_rr, bp_ii, cp_rr, cp_ii, a_re, a_im, h0_re, h0_im, dvec]
    if in_place:
        in_specs.append(pl.BlockSpec(memory_space=pl.ANY))
        args.append(y_all)
    return pl.pallas_call(
        functools.partial(_ssm_body, length=length, m=m, n_seq=n_seq, n_chunks=n_chunks,
                          in_place=in_place),
        grid=(SSM_NJ,),
        in_specs=in_specs,
        out_specs=[rows, st, st],
        out_shape=[jax.ShapeDtypeStruct((ROWS, D_MODEL), F32),
                   jax.ShapeDtypeStruct((n_seq, N_SSM_GROUPS * SSM_P), F32),
                   jax.ShapeDtypeStruct((n_seq, N_SSM_GROUPS * SSM_P), F32)],
        input_output_aliases={len(args) - 1: 0} if in_place else {},
        scratch_shapes=[pltpu.VMEM((length * LANES, LANES), BF16),
                        pltpu.VMEM((length * LANES, SSM_SQ * LANES), BF16),
                        pltpu.VMEM((length * LANES, SSM_SQ * LANES), BF16),
                        pltpu.VMEM((SSM_SQ, m, LANES), F32),
                        pltpu.VMEM((SSM_SQ, m, LANES), F32)],
        compiler_params=_cparams(("parallel",)),
        name="ssm_sample" if in_place else "ssm_prompt",
    )(*args)


def kernel(x_prompt, x_sample, cache_k, cache_v, state_conv, state_ssm_re, state_ssm_im, page_table, meta_tokens, norm_mix_pre, norm_mix_post, norm_ffn_pre, norm_ffn_post, w_in_even, lambda_q, lambda_k, subln_g, conv_w, conv_b, conv_ln_g, conv_ln_b, w_out_even, w_in_odd, ssm_a_re, ssm_a_im, ssm_b_re, ssm_b_im, ssm_c_re, ssm_c_im, ssm_d, ssm_log_dt, w_glu, w_out_odd, w_ffn_gate, w_ffn_up, w_ffn_down):
    n_pool = cache_k.shape[1]
    meta = jnp.broadcast_to(meta_tokens.astype(F32)[None], (BATCH, N_META, D_MODEL))
    x = jnp.concatenate([
        jnp.concatenate([meta, x_prompt], axis=1).reshape(ROWS_P, D_MODEL),
        x_sample.reshape(ROWS_S, D_MODEL),
        jnp.zeros((ROWS - ROWS_P - ROWS_S, D_MODEL), F32)], axis=0)
    slopes_vec = jnp.array([2.0 ** (-8.0 * (i + 1) / N_DH) for i in range(N_DH)], dtype=F32)
    slopes = jnp.broadcast_to(slopes_vec[:, None, None], (N_DH, 1, LANES))
    sample = slice(ROWS_P, ROWS_P + ROWS_S)

    ks, vs, convs_p, convs_s, sre_p, sim_p, sre_s, sim_s = [], [], [], [], [], [], [], []
    for layer in range(DEPTH):
        if layer % 2 == 0:
            e = layer // 2
            lam_init = 0.8 - 0.6 * math.exp(-0.3 * (2 * e))
            w_in = w_in_even[e].astype(BF16)
            hb = _rms_cast(x, norm_mix_pre[layer])
            q_p, q_t = _mm(hb, [w_in[:, 0:D_ATT]], split=True)
            k_p, k_t = _mm(hb, [w_in[:, D_ATT:2 * D_ATT]], split=True)
            v_p, v_t = _mm(hb, [w_in[:, 2 * D_ATT:3 * D_ATT]], split=True)
            g_p, g_t = _mm(hb, [w_in[:, 3 * D_ATT:3 * D_ATT + D_CONV], w_in[:, 3 * D_ATT + D_CONV:]],
                           split=True)
            per_seq = lambda a: a[:ROWS_S].reshape(DEC_BATCH, DEC_SEQ, a.shape[-1])
            pad_rows = jnp.zeros((ROWS_T - ROWS_S, D_ATT), F32)
            o_p = _prompt_attn(q_p, k_p, v_p, slopes, lambda_q[e], lambda_k[e], subln_g[e], lam_init)
            o_s = _decode_attn(page_table, per_seq(q_t), per_seq(k_t), per_seq(v_t),
                               jnp.transpose(cache_k[e], (0, 2, 3, 4, 1)).reshape(n_pool, D_ATT, PAGE_SIZE),
                               cache_v[e].reshape(n_pool, PAGE_SIZE * N_DH, DV),
                               slopes_vec, lambda_q[e], lambda_k[e], subln_g[e], lam_init)
            o_all = jnp.concatenate([o_p, o_s.reshape(ROWS_S, D_ATT), pad_rows], axis=0)
            ext_s = jnp.concatenate([state_conv[e], per_seq(g_t)], axis=1)
            c_p = _conv_prompt(g_p, jnp.zeros((BATCH, CONV_PRE, D_CONV), F32), conv_w[e], conv_b[e])
            c_s = _conv_sample(ext_s, conv_w[e], conv_b[e])
            c_all = jnp.concatenate([c_p, c_s, pad_rows], axis=0)
            w_out = w_out_even[e].astype(BF16)
            x = _even_out(o_all, c_all, conv_ln_g[e], conv_ln_b[e], w_out[:D_ATT], w_out[D_ATT:],
                          norm_mix_post[layer], x)
            ks.append((k_p.reshape(BATCH, T_P, N_DH, 2, DK), per_seq(k_t).reshape(DEC_BATCH, DEC_SEQ, N_DH, 2, DK)))
            vs.append((v_p.reshape(BATCH, T_P, N_DH, DV), per_seq(v_t).reshape(DEC_BATCH, DEC_SEQ, N_DH, DV)))
            convs_p.append(g_p.reshape(BATCH, T_P, D_CONV)[:, T_P - (CONV_W - 1):])
            convs_s.append(ext_s[:, DEC_SEQ:])
        else:
            o = layer // 2
            u_all = _mm(_rms_cast(x, norm_mix_pre[layer]), [w_in_odd[o].astype(BF16)])
            gp = N_SSM_GROUPS * SSM_P
            ap_re, ap_im, *ops = _ssm_params(
                ssm_a_re[o], ssm_a_im[o], ssm_log_dt[o], ssm_b_re[o], ssm_b_im[o],
                ssm_c_re[o], ssm_c_im[o])
            dvec = ssm_d[o].reshape(1, D_MODEL)
            zeros = jnp.zeros((BATCH, gp), F32)
            y_all, pr, pi = _ssm_apply(
                u_all, None, ops, ap_re[SSM_L:SSM_L + 1], ap_im[SSM_L:SSM_L + 1], zeros, zeros, dvec,
                length=SSM_L, m=SSM_M, n_seq=BATCH, n_chunks=SSM_NC, row_blk=0)
            y_all, sr, si = _ssm_apply(
                u_all, y_all, ops, ap_re[DEC_SEQ:DEC_SEQ + 1], ap_im[DEC_SEQ:DEC_SEQ + 1],
                state_ssm_re[o].reshape(DEC_BATCH, gp), state_ssm_im[o].reshape(DEC_BATCH, gp), dvec,
                length=DEC_SEQ, m=DEC_BATCH, n_seq=DEC_BATCH, n_chunks=1, row_blk=SAMPLE_BLK)
            x = _odd_out(y_all, x, w_glu[o].astype(BF16), w_out_odd[o].astype(BF16),
                         norm_mix_post[layer])
            sre_p.append(pr.reshape(BATCH, N_SSM_GROUPS, SSM_P))
            sim_p.append(pi.reshape(BATCH, N_SSM_GROUPS, SSM_P))
            sre_s.append(sr.reshape(DEC_BATCH, N_SSM_GROUPS, SSM_P))
            sim_s.append(si.reshape(DEC_BATCH, N_SSM_GROUPS, SSM_P))
        x = _ffn(x, norm_ffn_pre[layer], w_ffn_gate[layer].astype(BF16), w_ffn_up[layer].astype(BF16),
                 w_ffn_down[layer].astype(BF16), norm_ffn_post[layer])

    y_prompt = x[:ROWS_P].reshape(BATCH, T_P, D_MODEL)[:, N_META:]
    y_sample = x[sample].reshape(DEC_BATCH, DEC_SEQ, D_MODEL)
    return (y_prompt, y_sample,
            jnp.stack([k[0] for k in ks]), jnp.stack([v[0] for v in vs]),
            jnp.stack([k[1] for k in ks]), jnp.stack([v[1] for v in vs]),
            jnp.stack(convs_p), jnp.stack(convs_s),
            jnp.stack(sre_p), jnp.stack(sim_p), jnp.stack(sre_s), jnp.stack(sim_s))
```

```python
import functools
import math

import jax
import jax.numpy as jnp
from jax import lax
from jax.experimental import pallas as pl
from jax.experimental.pallas import tpu as pltpu

F32 = jnp.float32
BF16 = jnp.bfloat16

D_MODEL = 2048
BATCH = 4
SEQ = 2048
DEPTH = 2
DEC_BATCH = 8
DEC_SEQ = 4
PAGE_SIZE = 128
N_META = 16
D_ATT = D_MODEL // 2
N_DH = 8
DK = D_ATT // N_DH // 2
DV = 2 * DK
D_CONV = D_MODEL - D_ATT
CONV_W = 31
SSM_GROUP = 16
N_SSM_GROUPS = D_MODEL // SSM_GROUP
SSM_P = 64
D_FF = ((8 * D_MODEL // 3 + 255) // 256) * 256
EPS = 1e-6

T_P = N_META + SEQ
ROWS_P = BATCH * T_P
ROWS_S = DEC_BATCH * DEC_SEQ
TM = 640
ROWS = 8320
assert ROWS % TM == 0 and ROWS >= ROWS_P + ROWS_S
assert ROWS_P % ROWS_S == 0
SAMPLE_BLK = ROWS_P // ROWS_S

LANES = 128
SUBLANES = 8
VMEM_LIMIT = 56 * 1024 * 1024

SSM_L = 8
SSM_NC = T_P // SSM_L
SSM_M = ROWS // SSM_L
assert SSM_L >= DEC_SEQ
SSM_NJ = D_MODEL // LANES
SSM_G8 = LANES // SSM_GROUP
SSM_SW = SSM_G8 * SSM_P
assert T_P % SSM_L == 0 and ROWS % SSM_L == 0

ATT_T = 256
ATT_NT = SEQ // ATT_T
LOG2E = 1.4426950408889634
NT_DIMS = (((1,), (1,)), ((), ()))
DEC_PP = 8


def _cparams(sem, vmem=VMEM_LIMIT):
    return pltpu.CompilerParams(dimension_semantics=sem, vmem_limit_bytes=vmem)


def _rms(x, g):
    ms = jnp.mean(x * x, axis=-1, keepdims=True)
    return x * lax.rsqrt(ms + EPS) * g


def _rms_cast_body(x_ref, g_ref, o_ref):
    o_ref[...] = _rms(x_ref[...], g_ref[...]).astype(BF16)


def _rms_cast(x, g):
    rows, d = x.shape
    return pl.pallas_call(
        _rms_cast_body,
        grid=(rows // TM,),
        in_specs=[pl.BlockSpec((TM, d), lambda i: (i, 0)), pl.BlockSpec((1, d), lambda i: (0, 0))],
        out_specs=pl.BlockSpec((TM, d), lambda i: (i, 0)),
        out_shape=jax.ShapeDtypeStruct((rows, d), BF16),
        compiler_params=_cparams(("parallel",)),
        name="rms_cast",
    )(x, g.reshape(1, d))


ROWS_T = ROWS - ROWS_P
TAIL_AT = ROWS_P % TM
assert TAIL_AT + ROWS_T == TM


def _mm_body(h_ref, *refs, glu, split):
    n_w = 2 if glu else 1
    h = h_ref[...]
    a = jnp.dot(h, refs[0][...], preferred_element_type=F32)
    if glu:
        a = a * jax.nn.sigmoid(jnp.dot(h, refs[1][...], preferred_element_type=F32))
    refs[n_w][...] = a
    if split:
        refs[n_w + 1][...] = a[TAIL_AT:TM, :]


def _mm(h, ws, *, tn=1024, split=False):
    rows, d = h.shape
    n = ws[0].shape[1]
    glu = len(ws) == 2
    out_specs = [pl.BlockSpec((TM, tn), lambda i, j: (i, j))]
    out_shape = [jax.ShapeDtypeStruct((ROWS_P if split else rows, n), F32)]
    if split:
        out_specs.append(pl.BlockSpec((ROWS_T, tn), lambda i, j: (0, j)))
        out_shape.append(jax.ShapeDtypeStruct((ROWS_T, n), F32))
    outs = pl.pallas_call(
        functools.partial(_mm_body, glu=glu, split=split),
        grid=(rows // TM, n // tn),
        in_specs=[pl.BlockSpec((TM, d), lambda i, j: (i, 0))]
                 + [pl.BlockSpec((d, tn), lambda i, j: (0, j)) for _ in ws],
        out_specs=out_specs,
        out_shape=out_shape,
        compiler_params=_cparams(("arbitrary", "arbitrary")),
        name="mm_glu" if glu else "mm",
    )(h, *ws)
    return outs if split else outs[0]


def _mm_t_body(w_ref, h_ref, o_ref):
    o_ref[0] = lax.dot_general(w_ref[...], h_ref[...], NT_DIMS, preferred_element_type=F32)


def _mm_t(h, wt):
    n, d = wt.shape
    return pl.pallas_call(
        _mm_t_body,
        grid=(BATCH,),
        in_specs=[pl.BlockSpec((n, d), lambda b: (0, 0)),
                  pl.BlockSpec((T_P, d), lambda b: (b, 0))],
        out_specs=pl.BlockSpec((1, n, T_P), lambda b: (b, 0, 0)),
        out_shape=jax.ShapeDtypeStruct((BATCH, n, T_P), F32),
        compiler_params=_cparams(("arbitrary",)),
        name="mm_t",
    )(wt, h)


def _ffn_body(x_ref, gpre_ref, wg_ref, wu_ref, wd_ref, gpost_ref, *rest, split):
    o_ref, hb, acc = rest[0], rest[-2], rest[-1]
    j = pl.program_id(1)

    @pl.when(j == 0)
    def _():
        hb[...] = _rms(x_ref[...], gpre_ref[...]).astype(BF16)

    h = hb[...]
    gate = jnp.dot(h, wg_ref[...], preferred_element_type=F32)
    up = jnp.dot(h, wu_ref[...], preferred_element_type=F32)
    a = (jax.nn.silu(gate) * up).astype(BF16)
    part = jnp.dot(a, wd_ref[...], preferred_element_type=F32)

    @pl.when(j == 0)
    def _():
        acc[...] = part

    @pl.when(j > 0)
    def _():
        acc[...] += part

    @pl.when(j == pl.num_programs(1) - 1)
    def _():
        out = x_ref[...] + _rms(acc[...], gpost_ref[...])
        o_ref[...] = out
        if split:
            rest[1][...] = out[TAIL_AT:TM, :]


def _ffn(x, gpre, wg, wu, wd, gpost, *, tf=512, split=False):
    rows, d = x.shape
    dff = wg.shape[1]
    out_specs = [pl.BlockSpec((TM, d), lambda i, j: (i, 0))]
    out_shape = [jax.ShapeDtypeStruct((ROWS_P if split else rows, d), F32)]
    if split:
        out_specs.append(pl.BlockSpec((ROWS_T, d), lambda i, j: (0, 0)))
        out_shape.append(jax.ShapeDtypeStruct((ROWS_T, d), F32))
    outs = pl.pallas_call(
        functools.partial(_ffn_body, split=split),
        grid=(rows // TM, dff // tf),
        in_specs=[pl.BlockSpec((TM, d), lambda i, j: (i, 0)),
                  pl.BlockSpec((1, d), lambda i, j: (0, 0)),
                  pl.BlockSpec((d, tf), lambda i, j: (0, j)),
                  pl.BlockSpec((d, tf), lambda i, j: (0, j)),
                  pl.BlockSpec((tf, d), lambda i, j: (j, 0)),
                  pl.BlockSpec((1, d), lambda i, j: (0, 0))],
        out_specs=out_specs,
        out_shape=out_shape,
        scratch_shapes=[pltpu.VMEM((TM, d), BF16), pltpu.VMEM((TM, d), F32)],
        compiler_params=_cparams(("arbitrary", "arbitrary")),
        name="ffn",
    )(x, gpre.reshape(1, d), wg, wu, wd, gpost.reshape(1, d))
    return outs if split else outs[0]


def _odd_out_body(y_ref, res_ref, wglu_ref, wout_ref, gpost_ref, o_ref, yf, hb, acc, *, tf):
    j = pl.program_id(1)
    nj = pl.num_programs(1)

    @pl.when(j == 0)
    def _():
        for jj in range(yf.shape[0]):
            gy = jax.nn.gelu(y_ref[:, jj * tf:(jj + 1) * tf])
            yf[jj] = gy
            hb[:, jj * tf:(jj + 1) * tf] = gy.astype(BF16)

    t = jnp.dot(hb[...], wglu_ref[...], preferred_element_type=F32)
    a = (yf[j] * jax.nn.sigmoid(t)).astype(BF16)
    part = jnp.dot(a, wout_ref[...], preferred_element_type=F32)

    @pl.when(j == 0)
    def _():
        acc[...] = part

    @pl.when(j > 0)
    def _():
        acc[...] += part

    @pl.when(j == nj - 1)
    def _():
        o_ref[...] = res_ref[...] + _rms(acc[...], gpost_ref[...])


def _odd_out(y, res, wglu, wout, gpost, *, tf=512, tm=TM):
    rows, d = y.shape
    once = pl.Buffered(1)
    return pl.pallas_call(
        functools.partial(_odd_out_body, tf=tf),
        grid=(rows // tm, d // tf),
        in_specs=[pl.BlockSpec((tm, d), lambda i, j: (i, 0)),
                  pl.BlockSpec((tm, d), lambda i, j: (i, 0), pipeline_mode=once),
                  pl.BlockSpec((d, tf), lambda i, j: (0, j)),
                  pl.BlockSpec((tf, d), lambda i, j: (j, 0)),
                  pl.BlockSpec((1, d), lambda i, j: (0, 0))],
        out_specs=pl.BlockSpec((tm, d), lambda i, j: (i, 0)),
        out_shape=jax.ShapeDtypeStruct((rows, d), F32),
        scratch_shapes=[pltpu.VMEM((d // tf, tm, tf), F32), pltpu.VMEM((tm, d), BF16),
                        pltpu.VMEM((tm, d), F32)],
        compiler_params=_cparams(("parallel", "arbitrary")),
        name="odd_out",
    )(y, res, wglu, wout, gpost.reshape(1, d))


def _even_out_body(o_ref, c_ref, lng_ref, lnb_ref, wtop_ref, wbot_ref, gpost_ref, res_ref, out_ref):
    c = c_ref[...]
    mu = jnp.mean(c, axis=-1, keepdims=True)
    var = jnp.mean(jnp.square(c - mu), axis=-1, keepdims=True)
    cn = jax.nn.silu((c - mu) * lax.rsqrt(var + EPS) * lng_ref[...] + lnb_ref[...])
    y = jnp.dot(o_ref[...].astype(BF16), wtop_ref[...], preferred_element_type=F32)
    y = y + jnp.dot(cn.astype(BF16), wbot_ref[...], preferred_element_type=F32)
    out_ref[...] = res_ref[...] + _rms(y, gpost_ref[...])


def _even_out(o, c, lng, lnb, wtop, wbot, gpost, res):
    rows, d = res.shape
    da, dc = o.shape[1], c.shape[1]
    return pl.pallas_call(
        _even_out_body,
        grid=(rows // TM,),
        in_specs=[pl.BlockSpec((TM, da), lambda i: (i, 0)),
                  pl.BlockSpec((TM, dc), lambda i: (i, 0)),
                  pl.BlockSpec((1, dc), lambda i: (0, 0)),
                  pl.BlockSpec((1, dc), lambda i: (0, 0)),
                  pl.BlockSpec((da, d), lambda i: (0, 0)),
                  pl.BlockSpec((dc, d), lambda i: (0, 0)),
                  pl.BlockSpec((1, d), lambda i: (0, 0)),
                  pl.BlockSpec((TM, d), lambda i: (i, 0))],
        out_specs=pl.BlockSpec((TM, d), lambda i: (i, 0)),
        out_shape=jax.ShapeDtypeStruct((rows, d), F32),
        compiler_params=_cparams(("parallel",)),
        name="even_out",
    )(o, c, lng.reshape(1, dc), lnb.reshape(1, dc), wtop, wbot, gpost.reshape(1, d), res)


def _lam_from(lq_ref, lk_ref, lam_init):
    s0 = jnp.sum(lq_ref[0:1, :] * lk_ref[0:1, :], axis=-1, keepdims=True)
    s1 = jnp.sum(lq_ref[1:2, :] * lk_ref[1:2, :], axis=-1, keepdims=True)
    return jnp.exp(s0) - jnp.exp(s1) + lam_init


def _softmax_step(carry, q2, kc, vc, bias, k_is_transposed=False):
    m, l, acc = carry
    if k_is_transposed:
        s = jnp.dot(q2, kc, preferred_element_type=F32) + bias
    else:
        s = lax.dot_general(q2, kc, (((1,), (1,)), ((), ())), preferred_element_type=F32) + bias
    m_new = jnp.maximum(m, jnp.max(s, axis=-1, keepdims=True))
    alpha = jnp.exp(m - m_new)
    p = jnp.exp(s - m_new)
    l = alpha * l + jnp.sum(p, axis=-1, keepdims=True)
    acc = alpha * acc + jnp.dot(p.astype(BF16), vc, preferred_element_type=F32)
    return m_new, l, acc


def _stack_maps(q, scale):
    lane = lax.broadcasted_iota(jnp.int32, q.shape, 1)
    qs = q * scale
    q0 = jnp.where(lane < DK, qs, 0.0)
    q1 = jnp.where(lane >= DK, qs, 0.0)
    return jnp.concatenate([q0, q1], axis=0).astype(BF16)


def _diff_out(m, l, acc, n, lam, sg, lam_init):
    o = acc[:n] / l[:n] - lam * (acc[n:] / l[n:])
    return _rms(o, sg) * (1.0 - lam_init)


def _prompt_attn_body(slope_ref, lq_ref, lk_ref, sg_ref, q_ref, k_ref, v_ref, o_ref,
                      kb, vb, nb_s, *, lam_init):
    slope = slope_ref[0, 0:1, 0:1]
    lam = _lam_from(lq_ref, lk_ref, lam_init)
    sg = sg_ref[...]
    scale = DK ** -0.5
    kb[...] = k_ref[...].astype(BF16)
    vb[:, 0:DV] = v_ref[...].astype(BF16)
    neg_inf = float("-inf")

    zpad = jnp.zeros((LANES - N_META, LANES), BF16)
    k_meta = jnp.concatenate([kb[0:N_META, :], zpad], axis=0)
    v_meta = jnp.concatenate([vb[0:N_META, 0:DV], zpad], axis=0)

    rm = lax.broadcasted_iota(jnp.int32, (2 * N_META, LANES), 0) % N_META
    cm = lax.broadcasted_iota(jnp.int32, (2 * N_META, LANES), 1)
    bias_m = jnp.where(cm <= rm, -slope * (rm - cm).astype(F32), neg_inf)
    q2 = _stack_maps(q_ref[0:N_META, :], scale)
    init = (jnp.full((2 * N_META, 1), neg_inf, F32), jnp.zeros((2 * N_META, 1), F32),
            jnp.zeros((2 * N_META, LANES), F32))
    m, l, acc = _softmax_step(init, q2, k_meta, v_meta, bias_m)
    o_ref[0:N_META, :] = _diff_out(m, l, acc, N_META, lam, sg, lam_init)

    sl2 = slope * LOG2E
    vb[:, DV:2 * DV] = jnp.ones((T_P, DV), BF16)
    rr = lax.broadcasted_iota(jnp.int32, (2 * ATT_T, ATT_T), 0) % ATT_T
    cc = lax.broadcasted_iota(jnp.int32, (2 * ATT_T, ATT_T), 1)
    nbase = -sl2 * (rr - cc).astype(F32)
    nb_s[0] = nbase
    nb_s[1] = jnp.where(cc <= rr, nbase, neg_inf)
    nb_meta = jnp.where(cc < N_META, nbase, neg_inf)

    for i in range(ATT_NT):
        q0 = N_META + i * ATT_T
        q2 = _stack_maps(q_ref[q0:q0 + ATT_T, :], scale * LOG2E)
        tiles = [(0, None, float(N_META + i * ATT_T))]
        tiles += [(N_META + j * ATT_T, 0, float((i - j) * ATT_T)) for j in range(i)]
        tiles += [(q0, 1, 0.0)]
        scores, mx = [], jnp.full((2 * ATT_T, LANES), neg_inf, F32)
        for k0, kind, off in tiles:
            s = lax.dot_general(q2, kb[k0:k0 + ATT_T, :], NT_DIMS, preferred_element_type=F32)
            s = s + (nb_meta if kind is None else nb_s[kind])
            mx = jnp.maximum(mx, jnp.maximum(s[:, 0:LANES], s[:, LANES:2 * LANES]) - sl2 * off)
            scores.append(s)
        m = jnp.max(mx, axis=-1, keepdims=True)
        probs = []
        for (k0, kind, off), s in zip(tiles, scores):
            probs.append(jnp.exp2(s - (m + sl2 * off)).astype(BF16))
        p_all = jnp.concatenate(probs, axis=1)
        v_all = jnp.concatenate([vb[0:ATT_T, :], vb[N_META:q0 + ATT_T, :]], axis=0)
        acc = jnp.dot(p_all, v_all, preferred_element_type=F32)
        o0 = acc[0:ATT_T, 0:DV] / acc[0:ATT_T, DV:2 * DV]
        o1 = acc[ATT_T:2 * ATT_T, 0:DV] / acc[ATT_T:2 * ATT_T, DV:2 * DV]
        o_ref[q0:q0 + ATT_T, :] = _rms(o0 - lam * o1, sg) * (1.0 - lam_init)


def _prompt_attn(q_all, k_all, v_all, slopes, lq, lk, sg, lam_init):
    blk = pl.BlockSpec((T_P, DV), lambda b, h: (b, h))
    return pl.pallas_call(
        functools.partial(_prompt_attn_body, lam_init=lam_init),
        grid=(BATCH, N_DH),
        in_specs=[pl.BlockSpec((1, 1, LANES), lambda b, h: (h, 0, 0)),
                  pl.BlockSpec((2, DK), lambda b, h: (0, 0)),
                  pl.BlockSpec((2, DK), lambda b, h: (0, 0)),
                  pl.BlockSpec((1, DV), lambda b, h: (0, 0)),
                  blk, blk, blk],
        out_specs=blk,
        out_shape=jax.ShapeDtypeStruct((ROWS_P, D_ATT), F32),
        scratch_shapes=[pltpu.VMEM((T_P, DV), BF16),
                        pltpu.VMEM((T_P, 2 * DV), BF16),
                        pltpu.VMEM((2, 2 * ATT_T, ATT_T), F32)],
        compiler_params=_cparams(("parallel", "parallel")),
        name="prompt_attn",
    )(slopes, lq, lk, sg.reshape(1, DV), q_all, k_all, v_all)


def _decode_attn_body(pt_ref, srow_ref, qi_ref, lq_ref, lk_ref, sg_ref, q_ref, kn_ref, vn_ref, *refs,
                      lam_init, n_pages):
    k_refs, v_refs = refs[:DEC_PP], refs[DEC_PP:2 * DEC_PP]
    o_ref, m_s, l_s, acc_s = refs[2 * DEC_PP:]
    s_id = pl.program_id(1)
    n_rows = 2 * N_DH * DEC_SEQ
    past_len = n_pages * PAGE_SIZE
    neg_inf = float("-inf")

    row = lax.broadcasted_iota(jnp.int32, (n_rows, D_ATT), 0)
    lane = lax.broadcasted_iota(jnp.int32, (n_rows, D_ATT), 1)
    q2 = jnp.where(lane // DK == row // DEC_SEQ, q_ref[0] * (DK ** -0.5), 0.0).astype(BF16)
    srow = srow_ref[...]
    qi = qi_ref[...]
    col = lax.broadcasted_iota(jnp.int32, (n_rows, PAGE_SIZE), 1).astype(F32)
    srow_w = jnp.concatenate([srow] * DEC_PP, axis=1)
    qi_w = jnp.concatenate([qi] * DEC_PP, axis=1)
    col_w = lax.broadcasted_iota(jnp.int32, (n_rows, DEC_PP * PAGE_SIZE), 1).astype(F32)

    @pl.when(s_id == 0)
    def _():
        m_s[...] = jnp.full(m_s.shape, neg_inf, F32)
        l_s[...] = jnp.zeros(l_s.shape, F32)
        acc_s[...] = jnp.zeros(acc_s.shape, F32)

    def step(kc, vc, bias, k_is_transposed=False):
        m, l, acc = _softmax_step((m_s[:, 0:1], l_s[:, 0:1], acc_s[...]), q2, kc, vc, bias,
                                  k_is_transposed)
        m_s[...] = jnp.broadcast_to(m, m_s.shape)
        l_s[...] = jnp.broadcast_to(l, l_s.shape)
        acc_s[...] = acc

    kt = jnp.concatenate([k_refs[p][0].astype(BF16) for p in range(DEC_PP)], axis=1)
    vc = jnp.concatenate(
        [jnp.concatenate(
            [v_refs[p][0, pl.ds(h, PAGE_SIZE, stride=N_DH), :].astype(BF16) for h in range(N_DH)],
            axis=1) for p in range(DEC_PP)], axis=0)
    kpos0 = jnp.asarray(s_id * (DEC_PP * PAGE_SIZE), F32)
    dist = (past_len + qi_w) - (kpos0 + col_w)
    step(kt, vc, -srow_w * dist, k_is_transposed=True)

    @pl.when(s_id == pl.num_programs(1) - 1)
    def _():
        dist = qi - col
        bias = jnp.where(dist >= 0, -srow * dist, neg_inf)
        step(kn_ref[0].astype(BF16), vn_ref[0].astype(BF16), bias)
        lam = _lam_from(lq_ref, lk_ref, lam_init)
        sg = sg_ref[...]
        l = l_s[:, 0:1]
        for h in range(N_DH):
            blk = acc_s[h * 2 * DEC_SEQ:(h + 1) * 2 * DEC_SEQ, h * DV:(h + 1) * DV]
            blk = blk / l[h * 2 * DEC_SEQ:(h + 1) * 2 * DEC_SEQ]
            o = blk[0:DEC_SEQ] - lam * blk[DEC_SEQ:2 * DEC_SEQ]
            o_ref[0, :, h * DV:(h + 1) * DV] = _rms(o, sg) * (1.0 - lam_init)


def _decode_attn(page_table, q_s, k_s, v_s, cache_k, cache_v, slopes_vec, lq, lk, sg, lam_init):
    n_pages = page_table.shape[1]
    n_rows = 2 * N_DH * DEC_SEQ
    q_t = jnp.tile(q_s, (1, 2 * N_DH, 1))
    pad = ((0, 0), (0, PAGE_SIZE - DEC_SEQ), (0, 0))
    kn = jnp.pad(k_s, pad)
    vn = jnp.pad(v_s, pad)
    ridx = jnp.arange(n_rows)
    srow = jnp.broadcast_to(slopes_vec[ridx // (2 * DEC_SEQ)][:, None], (n_rows, PAGE_SIZE)).astype(F32)
    qi = jnp.broadcast_to((ridx % DEC_SEQ)[:, None], (n_rows, PAGE_SIZE)).astype(F32)

    def page_spec(p, shape):
        return pl.BlockSpec((1,) + shape, lambda b, s, pt: (pt[b, s * DEC_PP + p], 0, 0))

    const2 = lambda b, s, pt: (0, 0)
    per_b = lambda b, s, pt: (b, 0, 0)
    grid_spec = pltpu.PrefetchScalarGridSpec(
        num_scalar_prefetch=1,
        grid=(DEC_BATCH, n_pages // DEC_PP),
        in_specs=[pl.BlockSpec((n_rows, PAGE_SIZE), const2),
                  pl.BlockSpec((n_rows, PAGE_SIZE), const2),
                  pl.BlockSpec((2, DK), const2),
                  pl.BlockSpec((2, DK), const2),
                  pl.BlockSpec((1, DV), const2),
                  pl.BlockSpec((1, n_rows, D_ATT), per_b),
                  pl.BlockSpec((1, PAGE_SIZE, D_ATT), per_b),
                  pl.BlockSpec((1, PAGE_SIZE, D_ATT), per_b)]
                 + [page_spec(p, (D_ATT, PAGE_SIZE)) for p in range(DEC_PP)]
                 + [page_spec(p, (PAGE_SIZE * N_DH, DV)) for p in range(DEC_PP)],
        out_specs=pl.BlockSpec((1, DEC_SEQ, D_ATT), per_b),
        scratch_shapes=[pltpu.VMEM((n_rows, LANES), F32), pltpu.VMEM((n_rows, LANES), F32),
                        pltpu.VMEM((n_rows, D_ATT), F32)],
    )
    return pl.pallas_call(
        functools.partial(_decode_attn_body, lam_init=lam_init, n_pages=n_pages),
        grid_spec=grid_spec,
        out_shape=jax.ShapeDtypeStruct((DEC_BATCH, DEC_SEQ, D_ATT), F32),
        compiler_params=_cparams(("parallel", "arbitrary")),
        name="decode_attn",
    )(page_table, srow, qi, lq, lk, sg.reshape(1, DV), q_t, kn, vn,
      *([cache_k] * DEC_PP), *([cache_v] * DEC_PP))


CONV_PRE = 32
CONV_RC = 48
CONV_CT = 256
assert T_P % CONV_RC == 0


def _conv_prompt_body(prev_ref, g_ref, w_ref, b_ref, o_ref, gp, wr_s):
    gp[0:CONV_PRE, :] = prev_ref[0]
    gp[CONV_PRE:, :] = g_ref[...]
    lead = CONV_PRE - (CONV_W - 1)
    win_rows = CONV_RC + CONV_PRE
    bias = b_ref[...]

    def chunk(ci, _):
        t0 = pl.multiple_of(ci * CONV_RC, SUBLANES)
        win = gp[pl.ds(t0, win_rows), :]
        acc = jnp.broadcast_to(bias, (CONV_RC, CONV_CT))
        for r in range(SUBLANES):
            taps = [k for k in range(CONV_W) if (k + lead) % SUBLANES == r]
            if not taps:
                continue
            hi = max(k + lead for k in taps) - r + CONV_RC
            if r:
                wr_s[0:hi, :] = win[r:r + hi, :]
            for k in taps:
                a = k + lead - r
                src = wr_s[a:a + CONV_RC, :] if r else win[a:a + CONV_RC, :]
                acc = acc + w_ref[k:k + 1, :] * src
        o_ref[pl.ds(t0, CONV_RC), :] = acc
        return 0

    lax.fori_loop(0, T_P // CONV_RC, chunk, 0)


def _conv_prompt(g_all, prev, w, b):
    return pl.pallas_call(
        _conv_prompt_body,
        grid=(BATCH, D_CONV // CONV_CT),
        in_specs=[pl.BlockSpec((1, CONV_PRE, CONV_CT), lambda bi, j: (bi, 0, j)),
                  pl.BlockSpec((T_P, CONV_CT), lambda bi, j: (bi, j)),
                  pl.BlockSpec((CONV_W, CONV_CT), lambda bi, j: (0, j)),
                  pl.BlockSpec((1, CONV_CT), lambda bi, j: (0, j))],
        out_specs=pl.BlockSpec((T_P, CONV_CT), lambda bi, j: (bi, j)),
        out_shape=jax.ShapeDtypeStruct((ROWS_P, D_CONV), F32),
        scratch_shapes=[pltpu.VMEM((CONV_PRE + T_P, CONV_CT), F32),
                        pltpu.VMEM((CONV_RC + CONV_PRE, CONV_CT), F32)],
        compiler_params=_cparams(("parallel", "parallel")),
        name="conv_prompt",
    )(prev, g_all, w, b.reshape(1, D_CONV))


def _conv_sample_body(ext_ref, w_ref, b_ref, o_ref):
    w = w_ref[...]
    for bi in range(DEC_BATCH):
        for t in range(DEC_SEQ):
            acc = jnp.sum(w * ext_ref[bi, t:t + CONV_W, :], axis=0, keepdims=True) + b_ref[...]
            o_ref[bi * DEC_SEQ + t:bi * DEC_SEQ + t + 1, :] = acc


def _conv_sample(ext, w, b):
    return pl.pallas_call(
        _conv_sample_body,
        out_shape=jax.ShapeDtypeStruct((ROWS_S, D_CONV), F32),
        name="conv_sample",
    )(ext, w, b.reshape(1, D_CONV))


SSM_NK = SSM_L + 1


def _ssm_param_body(lre_ref, lim_ref, ldt_ref, bre_ref, bim_ref, cre_ref, cim_ref,
                    ap_re, ap_im, bp_rr, bp_ii, cp_rr, cp_ii, km_ref,
                    ar_s, ai_s, pr_s, pi_s, bbr_s, bbi_s):
    k = pl.program_id(0)

    @pl.when(k == 0)
    def _():
        lr, li = lre_ref[...], lim_ref[...]
        dt = jnp.exp(ldt_ref[...])
        er = jnp.exp(lr * dt)
        ar = er * jnp.cos(li * dt)
        ai = er * jnp.sin(li * dt)
        den = lr * lr + li * li
        xr, xi = ar - 1.0, ai
        cr = (xr * lr + xi * li) / den
        ci = (xi * lr - xr * li) / den
        br, bi = bre_ref[...], bim_ref[...]
        bbr_s[...] = cr * br - ci * bi
        bbi_s[...] = cr * bi + ci * br
        ar_s[...] = ar
        ai_s[...] = ai
        pr_s[...] = jnp.ones(pr_s.shape, F32)
        pi_s[...] = jnp.zeros(pi_s.shape, F32)

    @pl.when(k > 0)
    def _():
        pr, pi = pr_s[...], pi_s[...]
        ar, ai = ar_s[...], ai_s[...]
        pr_s[...] = pr * ar - pi * ai
        pi_s[...] = pr * ai + pi * ar

    pr, pi = pr_s[...], pi_s[...]
    ap_re[0] = pr
    ap_im[0] = pi
    bbr, bbi = bbr_s[...], bbi_s[...]
    bpr = pr * bbr - pi * bbi
    bpi = pr * bbi + pi * bbr
    bp_rr[0] = jnp.concatenate([bpr, bpr], axis=-1)
    bp_ii[0] = jnp.concatenate([bpi, bpi], axis=-1)
    cr, ci = cre_ref[...], cim_ref[...]
    cpr = pr * cr - pi * ci
    cpi = -(pr * ci + pi * cr)
    cp_rr[0] = jnp.concatenate([cpr, cpr], axis=-1)
    cp_ii[0] = jnp.concatenate([cpi, cpi], axis=-1)
    dn = (((2,), (2,)), ((0,), (0,)))
    hp = lax.Precision.HIGHEST
    km_ref[0] = (lax.dot_general(bpr, cr, dn, precision=hp, preferred_element_type=F32)
                 - lax.dot_general(bpi, ci, dn, precision=hp, preferred_element_type=F32))


def _ssm_params(lre, lim, log_dt, b_re, b_im, c_re, c_im):
    g, p, c = N_SSM_GROUPS, SSM_P, SSM_GROUP
    bc = lambda a: jnp.broadcast_to(a[:, None, :], (g, c, p))
    ldt = jnp.broadcast_to(log_dt[:, None, None], (g, c, p))
    bt_re = jnp.swapaxes(b_re, 1, 2)
    bt_im = jnp.swapaxes(b_im, 1, 2)
    gcp = pl.BlockSpec((g, c, p), lambda k: (0, 0, 0))
    o_gcp = pl.BlockSpec((1, g, c, p), lambda k: (k, 0, 0, 0))
    o_wide = pl.BlockSpec((1, g, c, 2 * p), lambda k: (k, 0, 0, 0))
    narrow = jax.ShapeDtypeStruct((SSM_NK, g, c, p), F32)
    wide = jax.ShapeDtypeStruct((SSM_NK, g, c, 2 * p), F32)
    ap_re, ap_im, bp_rr, bp_ii, cp_rr, cp_ii, km = pl.pallas_call(
        _ssm_param_body,
        grid=(SSM_NK,),
        in_specs=[gcp] * 7,
        out_specs=[o_gcp] * 2 + [o_wide] * 4 + [pl.BlockSpec((1, g, c, c), lambda k: (k, 0, 0, 0))],
        out_shape=[narrow] * 2 + [wide] * 4 + [jax.ShapeDtypeStruct((SSM_NK, g, c, c), F32)],
        scratch_shapes=[pltpu.VMEM((g, c, p), F32)] * 6,
        compiler_params=_cparams(("arbitrary",)),
        name="ssm_params",
    )(bc(lre), bc(lim), ldt, bt_re, bt_im, c_re, c_im)
    flat = lambda a: a.reshape(SSM_NK, g * c, 2 * p)
    return (ap_re[:, :, 0, :].reshape(SSM_NK, g * p), ap_im[:, :, 0, :].reshape(SSM_NK, g * p),
            flat(bp_rr), flat(bp_ii), flat(cp_rr), flat(cp_ii), km.reshape(SSM_NK, g * c, c))


def _gather_chunks(u_ref, length, m):
    xs = [u_ref[pl.ds(s, m, stride=length), :].astype(BF16) for s in range(length)]
    return jnp.concatenate(xs, axis=1)


SSM_SQ = 2 * SSM_SW // LANES
SSM_HQ = SSM_SQ // 2


def _ssm_body(u_ref, km_ref, bpr_ref, bpi_ref, cpr_ref, cpi_ref, are_ref, aim_ref, h0r_ref, h0i_ref,
              d_ref, *rest, length, m, n_seq, n_chunks, in_place):
    if in_place:
        rest = rest[1:]
    o_ref, finr_ref, fini_ref, v_s, ws_s, wct_s, s_s, hp_s = rest
    x = _gather_chunks(u_ref, length, m)

    erow = lax.broadcasted_iota(jnp.int32, (SSM_GROUP, LANES), 0)
    elane = lax.broadcasted_iota(jnp.int32, (SSM_GROUP, LANES), 1)
    spread = jnp.where(elane % SSM_GROUP == erow, 1.0, 0.0).astype(BF16)
    rgrp = lax.broadcasted_iota(jnp.int32, (LANES, LANES), 0) // SSM_GROUP
    lane = lax.broadcasted_iota(jnp.int32, (LANES, LANES), 1)
    for q in range(length):
        blk = jnp.dot(km_ref[length - 1 - q].astype(BF16), spread, preferred_element_type=F32)
        v_s[q * LANES:(q + 1) * LANES, :] = jnp.where(rgrp == lane // SSM_GROUP, blk, 0.0).astype(BF16)
    for s in range(length):
        rows = slice(s * LANES, (s + 1) * LANES)
        for q in range(SSM_HQ):
            own = rgrp == 2 * q + lane // SSM_P
            re_t = slice(q * LANES, (q + 1) * LANES)
            im_t = slice((SSM_HQ + q) * LANES, (SSM_HQ + q + 1) * LANES)
            ws_s[rows, re_t] = jnp.where(own, bpr_ref[length - 1 - s], 0.0).astype(BF16)
            ws_s[rows, im_t] = jnp.where(own, bpi_ref[length - 1 - s], 0.0).astype(BF16)
            wct_s[rows, re_t] = jnp.where(own, cpr_ref[s + 1], 0.0).astype(BF16)
            wct_s[rows, im_t] = jnp.where(own, cpi_ref[s + 1], 0.0).astype(BF16)

    s_loc = jnp.dot(x, ws_s[...], preferred_element_type=F32)
    for q in range(SSM_SQ):
        s_s[q] = s_loc[:, q * LANES:(q + 1) * LANES]

    if m != n_seq * n_chunks:
        hp_s[...] = jnp.zeros(hp_s.shape, F32)
    ar = [are_ref[:, q * LANES:(q + 1) * LANES] for q in range(SSM_HQ)]
    ai = [aim_ref[:, q * LANES:(q + 1) * LANES] for q in range(SSM_HQ)]

    def advance(rows, hs):
        out_r, out_i = [], []
        for q in range(SSM_HQ):
            hr, hi = hs[q], hs[SSM_HQ + q]
            hp_s[q, rows, :] = hr
            hp_s[SSM_HQ + q, rows, :] = hi
            out_r.append(ar[q] * hr - ai[q] * hi + s_s[q, rows, :])
            out_i.append(ar[q] * hi + ai[q] * hr + s_s[SSM_HQ + q, rows, :])
        return tuple(out_r + out_i)

    hs = tuple([h0r_ref[:, q * LANES:(q + 1) * LANES] for q in range(SSM_HQ)]
               + [h0i_ref[:, q * LANES:(q + 1) * LANES] for q in range(SSM_HQ)])
    if n_chunks == 1:
        hs = advance(pl.ds(0, n_seq), hs)
    else:
        hs = lax.fori_loop(0, n_chunks,
                           lambda c, hs: advance(pl.ds(c, n_seq, stride=n_chunks), hs), hs)
    for q in range(SSM_HQ):
        finr_ref[:, q * LANES:(q + 1) * LANES] = hs[q]
        fini_ref[:, q * LANES:(q + 1) * LANES] = hs[SSM_HQ + q]

    hp = jnp.concatenate([hp_s[q].astype(BF16) for q in range(SSM_SQ)], axis=1)
    d = d_ref[...]
    for t in range(length):
        rows = pl.ds(t, m, stride=length)
        y = jnp.dot(x[:, :(t + 1) * LANES], v_s[(length - 1 - t) * LANES:, :],
                    preferred_element_type=F32)
        y = y + lax.dot_general(hp, wct_s[t * LANES:(t + 1) * LANES, :], NT_DIMS,
                                preferred_element_type=F32)
        o_ref[rows, :] = y + d * u_ref[rows, :]


def _ssm_apply(u_all, y_all, ops, a_re, a_im, h0_re, h0_im, dvec, *, length, m, n_seq, n_chunks,
               row_blk):
    in_place = y_all is not None
    bp_rr, bp_ii, cp_rr, cp_ii, km = ops
    rows = pl.BlockSpec((m * length, LANES), lambda j: (row_blk, j))
    comp = lambda last: pl.BlockSpec((SSM_NK, LANES, last), lambda j: (0, j, 0))
    vec = pl.BlockSpec((1, SSM_SW), lambda j: (0, j))
    st = pl.BlockSpec((n_seq, SSM_SW), lambda j: (0, j))
    in_specs = [rows, comp(SSM_GROUP)] + [comp(LANES)] * 4 + [vec, vec, st, st,
                pl.BlockSpec((1, LANES), lambda j: (0, j))]
    args = [u_all, km, bp_rr, bp_ii, cp_rr, cp_ii, a_re, a_im, h0_re, h0_im, dvec]
    if in_place:
        in_specs.append(pl.BlockSpec(memory_space=pl.ANY))
        args.append(y_all)
    return pl.pallas_call(
        functools.partial(_ssm_body, length=length, m=m, n_seq=n_seq, n_chunks=n_chunks,
                          in_place=in_place),
        grid=(SSM_NJ,),
        in_specs=in_specs,
        out_specs=[rows, st, st],
        out_shape=[jax.ShapeDtypeStruct((ROWS, D_MODEL), F32),
                   jax.ShapeDtypeStruct((n_seq, N_SSM_GROUPS * SSM_P), F32),
                   jax.ShapeDtypeStruct((n_seq, N_SSM_GROUPS * SSM_P), F32)],
        input_output_aliases={len(args) - 1: 0} if in_place else {},
        scratch_shapes=[pltpu.VMEM((length * LANES, LANES), BF16),
                        pltpu.VMEM((length * LANES, SSM_SQ * LANES), BF16),
                        pltpu.VMEM((length * LANES, SSM_SQ * LANES), BF16),
                        pltpu.VMEM((SSM_SQ, m, LANES), F32),
                        pltpu.VMEM((SSM_SQ, m, LANES), F32)],
        compiler_params=_cparams(("parallel",)),
        name="ssm_sample" if in_place else "ssm_prompt",
    )(*args)


def kernel(x_prompt, x_sample, cache_k, cache_v, state_conv, state_ssm_re, state_ssm_im, page_table, meta_tokens, norm_mix_pre, norm_mix_post, norm_ffn_pre, norm_ffn_post, w_in_even, lambda_q, lambda_k, subln_g, conv_w, conv_b, conv_ln_g, conv_ln_b, w_out_even, w_in_odd, ssm_a_re, ssm_a_im, ssm_b_re, ssm_b_im, ssm_c_re, ssm_c_im, ssm_d, ssm_log_dt, w_glu, w_out_odd, w_ffn_gate, w_ffn_up, w_ffn_down):
    n_pool = cache_k.shape[1]
    meta = jnp.broadcast_to(meta_tokens.astype(F32)[None], (BATCH, N_META, D_MODEL))
    x = jnp.concatenate([
        jnp.concatenate([meta, x_prompt], axis=1).reshape(ROWS_P, D_MODEL),
        x_sample.reshape(ROWS_S, D_MODEL),
        jnp.zeros((ROWS - ROWS_P - ROWS_S, D_MODEL), F32)], axis=0)
    slopes_vec = jnp.array([2.0 ** (-8.0 * (i + 1) / N_DH) for i in range(N_DH)], dtype=F32)
    slopes = jnp.broadcast_to(slopes_vec[:, None, None], (N_DH, 1, LANES))

    ks, vs, convs_p, convs_s, sre_p, sim_p, sre_s, sim_s = [], [], [], [], [], [], [], []
    for layer in range(DEPTH):
        if layer % 2 == 0:
            e = layer // 2
            lam_init = 0.8 - 0.6 * math.exp(-0.3 * (2 * e))
            w_in = w_in_even[e].astype(BF16)
            hb = _rms_cast(x, norm_mix_pre[layer])
            q_p, q_t = _mm(hb, [w_in[:, 0:D_ATT]], split=True)
            k_p, k_t = _mm(hb, [w_in[:, D_ATT:2 * D_ATT]], split=True)
            v_p, v_t = _mm(hb, [w_in[:, 2 * D_ATT:3 * D_ATT]], split=True)
            g_p, g_t = _mm(hb, [w_in[:, 3 * D_ATT:3 * D_ATT + D_CONV], w_in[:, 3 * D_ATT + D_CONV:]],
                           split=True)
            per_seq = lambda a: a[:ROWS_S].reshape(DEC_BATCH, DEC_SEQ, a.shape[-1])
            pad_rows = jnp.zeros((ROWS_T - ROWS_S, D_ATT), F32)
            o_p = _prompt_attn(q_p, k_p, v_p, slopes, lambda_q[e], lambda_k[e], subln_g[e], lam_init)
            o_s = _decode_attn(page_table, per_seq(q_t), per_seq(k_t), per_seq(v_t),
                               jnp.transpose(cache_k[e], (0, 2, 3, 4, 1)).reshape(n_pool, D_ATT, PAGE_SIZE),
                               cache_v[e].reshape(n_pool, PAGE_SIZE * N_DH, DV),
                               slopes_vec, lambda_q[e], lambda_k[e], subln_g[e], lam_init)
            o_all = jnp.concatenate([o_p, o_s.reshape(ROWS_S, D_ATT), pad_rows], axis=0)
            ext_s = jnp.concatenate([state_conv[e], per_seq(g_t)], axis=1)
            c_p = _conv_prompt(g_p, jnp.zeros((BATCH, CONV_PRE, D_CONV), F32), conv_w[e], conv_b[e])
            c_s = _conv_sample(ext_s, conv_w[e], conv_b[e])
            c_all = jnp.concatenate([c_p, c_s, pad_rows], axis=0)
            w_out = w_out_even[e].astype(BF16)
            x = _even_out(o_all, c_all, conv_ln_g[e], conv_ln_b[e], w_out[:D_ATT], w_out[D_ATT:],
                          norm_mix_post[layer], x)
            k_tok_minor = _mm_t(hb, jnp.transpose(w_in[:, D_ATT:2 * D_ATT]))
            k_prompt = jnp.transpose(k_tok_minor.reshape(BATCH, N_DH, 2, DK, T_P), (0, 4, 1, 2, 3))
            ks.append((k_prompt, per_seq(k_t).reshape(DEC_BATCH, DEC_SEQ, N_DH, 2, DK)))
            vs.append((v_p.reshape(BATCH, T_P, N_DH, DV), per_seq(v_t).reshape(DEC_BATCH, DEC_SEQ, N_DH, DV)))
            convs_p.append(g_p.reshape(BATCH, T_P, D_CONV)[:, T_P - (CONV_W - 1):])
            convs_s.append(ext_s[:, DEC_SEQ:])
        else:
            o = layer // 2
            u_all = _mm(_rms_cast(x, norm_mix_pre[layer]), [w_in_odd[o].astype(BF16)])
            gp = N_SSM_GROUPS * SSM_P
            ap_re, ap_im, *ops = _ssm_params(
                ssm_a_re[o], ssm_a_im[o], ssm_log_dt[o], ssm_b_re[o], ssm_b_im[o],
                ssm_c_re[o], ssm_c_im[o])
            dvec = ssm_d[o].reshape(1, D_MODEL)
            zeros = jnp.zeros((BATCH, gp), F32)
            y_all, pr, pi = _ssm_apply(
                u_all, None, ops, ap_re[SSM_L:SSM_L + 1], ap_im[SSM_L:SSM_L + 1], zeros, zeros, dvec,
                length=SSM_L, m=SSM_M, n_seq=BATCH, n_chunks=SSM_NC, row_blk=0)
            y_all, sr, si = _ssm_apply(
                u_all, y_all, ops, ap_re[DEC_SEQ:DEC_SEQ + 1], ap_im[DEC_SEQ:DEC_SEQ + 1],
                state_ssm_re[o].reshape(DEC_BATCH, gp), state_ssm_im[o].reshape(DEC_BATCH, gp), dvec,
                length=DEC_SEQ, m=DEC_BATCH, n_seq=DEC_BATCH, n_chunks=1, row_blk=SAMPLE_BLK)
            x = _odd_out(y_all, x, w_glu[o].astype(BF16), w_out_odd[o].astype(BF16),
                         norm_mix_post[layer])
            sre_p.append(pr.reshape(BATCH, N_SSM_GROUPS, SSM_P))
            sim_p.append(pi.reshape(BATCH, N_SSM_GROUPS, SSM_P))
            sre_s.append(sr.reshape(DEC_BATCH, N_SSM_GROUPS, SSM_P))
            sim_s.append(si.reshape(DEC_BATCH, N_SSM_GROUPS, SSM_P))
        x = _ffn(x, norm_ffn_pre[layer], w_ffn_gate[layer].astype(BF16), w_ffn_up[layer].astype(BF16),
                 w_ffn_down[layer].astype(BF16), norm_ffn_post[layer], split=layer == DEPTH - 1)

    x_p, x_t = x
    y_prompt = x_p.reshape(BATCH, T_P, D_MODEL)[:, N_META:]
    y_sample = x_t[:ROWS_S].reshape(DEC_BATCH, DEC_SEQ, D_MODEL)
    return (y_prompt, y_sample,
            jnp.stack([k[0] for k in ks]), jnp.stack([v[0] for v in vs]),
            jnp.stack([k[1] for k in ks]), jnp.stack([v[1] for v in vs]),
            jnp.stack(convs_p), jnp.stack(convs_s),
            jnp.stack(sre_p), jnp.stack(sim_p), jnp.stack(sre_s), jnp.stack(sim_s))
```

```python
import functools
import math

import jax
import jax.numpy as jnp
from jax import lax
from jax.experimental import pallas as pl
from jax.experimental.pallas import tpu as pltpu

F32 = jnp.float32
BF16 = jnp.bfloat16

D_MODEL = 2048
BATCH = 4
SEQ = 2048
DEPTH = 2
DEC_BATCH = 8
DEC_SEQ = 4
PAGE_SIZE = 128
N_META = 16
D_ATT = D_MODEL // 2
N_DH = 8
DK = D_ATT // N_DH // 2
DV = 2 * DK
D_CONV = D_MODEL - D_ATT
CONV_W = 31
SSM_GROUP = 16
N_SSM_GROUPS = D_MODEL // SSM_GROUP
SSM_P = 64
D_FF = ((8 * D_MODEL // 3 + 255) // 256) * 256
EPS = 1e-6

T_P = N_META + SEQ
ROWS_P = BATCH * T_P
ROWS_S = DEC_BATCH * DEC_SEQ
TM = 640
ROWS = 8320
assert ROWS % TM == 0 and ROWS >= ROWS_P + ROWS_S
assert ROWS_P % ROWS_S == 0
SAMPLE_BLK = ROWS_P // ROWS_S

LANES = 128
SUBLANES = 8
VMEM_LIMIT = 56 * 1024 * 1024

SSM_L = 8
SSM_NC = T_P // SSM_L
SSM_M = ROWS // SSM_L
assert SSM_L >= DEC_SEQ
SSM_NJ = D_MODEL // LANES
SSM_G8 = LANES // SSM_GROUP
SSM_SW = SSM_G8 * SSM_P
assert T_P % SSM_L == 0 and ROWS % SSM_L == 0

ATT_T = 256
ATT_NT = SEQ // ATT_T
LOG2E = 1.4426950408889634
NT_DIMS = (((1,), (1,)), ((), ()))
DEC_PP = 8


def _cparams(sem, vmem=VMEM_LIMIT):
    return pltpu.CompilerParams(dimension_semantics=sem, vmem_limit_bytes=vmem)


def _rms(x, g):
    ms = jnp.mean(x * x, axis=-1, keepdims=True)
    return x * lax.rsqrt(ms + EPS) * g


def _rms_cast_body(x_ref, g_ref, o_ref):
    o_ref[...] = _rms(x_ref[...], g_ref[...]).astype(BF16)


def _rms_cast(x, g):
    rows, d = x.shape
    return pl.pallas_call(
        _rms_cast_body,
        grid=(rows // TM,),
        in_specs=[pl.BlockSpec((TM, d), lambda i: (i, 0)), pl.BlockSpec((1, d), lambda i: (0, 0))],
        out_specs=pl.BlockSpec((TM, d), lambda i: (i, 0)),
        out_shape=jax.ShapeDtypeStruct((rows, d), BF16),
        compiler_params=_cparams(("parallel",)),
        name="rms_cast",
    )(x, g.reshape(1, d))


ROWS_T = ROWS - ROWS_P
TAIL_AT = ROWS_P % TM
assert TAIL_AT + ROWS_T == TM


def _mm_body(h_ref, *refs, glu, split):
    n_w = 2 if glu else 1
    h = h_ref[...]
    a = jnp.dot(h, refs[0][...], preferred_element_type=F32)
    if glu:
        a = a * jax.nn.sigmoid(jnp.dot(h, refs[1][...], preferred_element_type=F32))
    refs[n_w][...] = a
    if split:
        refs[n_w + 1][...] = a[TAIL_AT:TM, :]


def _mm(h, ws, *, tn=1024, split=False):
    rows, d = h.shape
    n = ws[0].shape[1]
    glu = len(ws) == 2
    out_specs = [pl.BlockSpec((TM, tn), lambda i, j: (i, j))]
    out_shape = [jax.ShapeDtypeStruct((ROWS_P if split else rows, n), F32)]
    if split:
        out_specs.append(pl.BlockSpec((ROWS_T, tn), lambda i, j: (0, j)))
        out_shape.append(jax.ShapeDtypeStruct((ROWS_T, n), F32))
    outs = pl.pallas_call(
        functools.partial(_mm_body, glu=glu, split=split),
        grid=(rows // TM, n // tn),
        in_specs=[pl.BlockSpec((TM, d), lambda i, j: (i, 0))]
                 + [pl.BlockSpec((d, tn), lambda i, j: (0, j)) for _ in ws],
        out_specs=out_specs,
        out_shape=out_shape,
        compiler_params=_cparams(("arbitrary", "arbitrary")),
        name="mm_glu" if glu else "mm",
    )(h, *ws)
    return outs if split else outs[0]


def _mm_t_body(w_ref, h_ref, o_ref):
    o_ref[0] = lax.dot_general(w_ref[...], h_ref[...], NT_DIMS, preferred_element_type=F32)


def _mm_t(h, wt):
    n, d = wt.shape
    return pl.pallas_call(
        _mm_t_body,
        grid=(BATCH,),
        in_specs=[pl.BlockSpec((n, d), lambda b: (0, 0)),
                  pl.BlockSpec((T_P, d), lambda b: (b, 0))],
        out_specs=pl.BlockSpec((1, n, T_P), lambda b: (b, 0, 0)),
        out_shape=jax.ShapeDtypeStruct((BATCH, n, T_P), F32),
        compiler_params=_cparams(("arbitrary",)),
        name="mm_t",
    )(wt, h)


def _ffn_body(x_ref, gpre_ref, wg_ref, wu_ref, wd_ref, gpost_ref, *rest, split):
    o_ref, hb, acc = rest[0], rest[-2], rest[-1]
    j = pl.program_id(1)

    @pl.when(j == 0)
    def _():
        hb[...] = _rms(x_ref[...], gpre_ref[...]).astype(BF16)

    h = hb[...]
    gate = jnp.dot(h, wg_ref[...], preferred_element_type=F32)
    up = jnp.dot(h, wu_ref[...], preferred_element_type=F32)
    a = (jax.nn.silu(gate) * up).astype(BF16)
    part = jnp.dot(a, wd_ref[...], preferred_element_type=F32)

    @pl.when(j == 0)
    def _():
        acc[...] = part

    @pl.when(j > 0)
    def _():
        acc[...] += part

    @pl.when(j == pl.num_programs(1) - 1)
    def _():
        out = x_ref[...] + _rms(acc[...], gpost_ref[...])
        o_ref[...] = out
        if split:
            rest[1][...] = out[TAIL_AT:TM, :]


def _ffn(x, gpre, wg, wu, wd, gpost, *, tf=512, split=False):
    rows, d = x.shape
    dff = wg.shape[1]
    out_specs = [pl.BlockSpec((TM, d), lambda i, j: (i, 0))]
    out_shape = [jax.ShapeDtypeStruct((ROWS_P if split else rows, d), F32)]
    if split:
        out_specs.append(pl.BlockSpec((ROWS_T, d), lambda i, j: (0, 0)))
        out_shape.append(jax.ShapeDtypeStruct((ROWS_T, d), F32))
    outs = pl.pallas_call(
        functools.partial(_ffn_body, split=split),
        grid=(rows // TM, dff // tf),
        in_specs=[pl.BlockSpec((TM, d), lambda i, j: (i, 0)),
                  pl.BlockSpec((1, d), lambda i, j: (0, 0)),
                  pl.BlockSpec((d, tf), lambda i, j: (0, j)),
                  pl.BlockSpec((d, tf), lambda i, j: (0, j)),
                  pl.BlockSpec((tf, d), lambda i, j: (j, 0)),
                  pl.BlockSpec((1, d), lambda i, j: (0, 0))],
        out_specs=out_specs,
        out_shape=out_shape,
        scratch_shapes=[pltpu.VMEM((TM, d), BF16), pltpu.VMEM((TM, d), F32)],
        compiler_params=_cparams(("arbitrary", "arbitrary")),
        name="ffn",
    )(x, gpre.reshape(1, d), wg, wu, wd, gpost.reshape(1, d))
    return outs if split else outs[0]


def _odd_out_body(y_ref, res_ref, wglu_ref, wout_ref, gpost_ref, o_ref, yf, hb, acc, *, tf):
    j = pl.program_id(1)
    nj = pl.num_programs(1)

    @pl.when(j == 0)
    def _():
        for jj in range(yf.shape[0]):
            gy = jax.nn.gelu(y_ref[:, jj * tf:(jj + 1) * tf])
            yf[jj] = gy
            hb[:, jj * tf:(jj + 1) * tf] = gy.astype(BF16)

    t = jnp.dot(hb[...], wglu_ref[...], preferred_element_type=F32)
    a = (yf[j] * jax.nn.sigmoid(t)).astype(BF16)
    part = jnp.dot(a, wout_ref[...], preferred_element_type=F32)

    @pl.when(j == 0)
    def _():
        acc[...] = part

    @pl.when(j > 0)
    def _():
        acc[...] += part

    @pl.when(j == nj - 1)
    def _():
        o_ref[...] = res_ref[...] + _rms(acc[...], gpost_ref[...])


def _odd_out(y, res, wglu, wout, gpost, *, tf=512, tm=TM):
    rows, d = y.shape
    once = pl.Buffered(1)
    return pl.pallas_call(
        functools.partial(_odd_out_body, tf=tf),
        grid=(rows // tm, d // tf),
        in_specs=[pl.BlockSpec((tm, d), lambda i, j: (i, 0)),
                  pl.BlockSpec((tm, d), lambda i, j: (i, 0), pipeline_mode=once),
                  pl.BlockSpec((d, tf), lambda i, j: (0, j)),
                  pl.BlockSpec((tf, d), lambda i, j: (j, 0)),
                  pl.BlockSpec((1, d), lambda i, j: (0, 0))],
        out_specs=pl.BlockSpec((tm, d), lambda i, j: (i, 0)),
        out_shape=jax.ShapeDtypeStruct((rows, d), F32),
        scratch_shapes=[pltpu.VMEM((d // tf, tm, tf), F32), pltpu.VMEM((tm, d), BF16),
                        pltpu.VMEM((tm, d), F32)],
        compiler_params=_cparams(("parallel", "arbitrary")),
        name="odd_out",
    )(y, res, wglu, wout, gpost.reshape(1, d))


def _even_out_body(o_ref, c_ref, ot_ref, ct_ref, lng_ref, lnb_ref, wtop_ref, wbot_ref, gpost_ref,
                   res_ref, out_ref):
    def emit(o, c):
        mu = jnp.mean(c, axis=-1, keepdims=True)
        var = jnp.mean(jnp.square(c - mu), axis=-1, keepdims=True)
        cn = jax.nn.silu((c - mu) * lax.rsqrt(var + EPS) * lng_ref[...] + lnb_ref[...])
        y = jnp.dot(o.astype(BF16), wtop_ref[...], preferred_element_type=F32)
        y = y + jnp.dot(cn.astype(BF16), wbot_ref[...], preferred_element_type=F32)
        out_ref[...] = res_ref[...] + _rms(y, gpost_ref[...])

    last = pl.num_programs(0) - 1

    @pl.when(pl.program_id(0) < last)
    def _():
        emit(o_ref[...], c_ref[...])

    @pl.when(pl.program_id(0) == last)
    def _():
        emit(jnp.concatenate([o_ref[0:TAIL_AT, :], ot_ref[...]], axis=0),
             jnp.concatenate([c_ref[0:TAIL_AT, :], ct_ref[...]], axis=0))


def _even_out(o, c, o_tail, c_tail, lng, lnb, wtop, wbot, gpost, res):
    rows, d = res.shape
    da, dc = o.shape[1], c.shape[1]
    return pl.pallas_call(
        _even_out_body,
        grid=(rows // TM,),
        in_specs=[pl.BlockSpec((TM, da), lambda i: (i, 0)),
                  pl.BlockSpec((TM, dc), lambda i: (i, 0)),
                  pl.BlockSpec((ROWS_T, da), lambda i: (0, 0)),
                  pl.BlockSpec((ROWS_T, dc), lambda i: (0, 0)),
                  pl.BlockSpec((1, dc), lambda i: (0, 0)),
                  pl.BlockSpec((1, dc), lambda i: (0, 0)),
                  pl.BlockSpec((da, d), lambda i: (0, 0)),
                  pl.BlockSpec((dc, d), lambda i: (0, 0)),
                  pl.BlockSpec((1, d), lambda i: (0, 0)),
                  pl.BlockSpec((TM, d), lambda i: (i, 0))],
        out_specs=pl.BlockSpec((TM, d), lambda i: (i, 0)),
        out_shape=jax.ShapeDtypeStruct((rows, d), F32),
        compiler_params=_cparams(("parallel",)),
        name="even_out",
    )(o, c, o_tail, c_tail, lng.reshape(1, dc), lnb.reshape(1, dc), wtop, wbot, gpost.reshape(1, d),
      res)


def _lam_from(lq_ref, lk_ref, lam_init):
    s0 = jnp.sum(lq_ref[0:1, :] * lk_ref[0:1, :], axis=-1, keepdims=True)
    s1 = jnp.sum(lq_ref[1:2, :] * lk_ref[1:2, :], axis=-1, keepdims=True)
    return jnp.exp(s0) - jnp.exp(s1) + lam_init


def _softmax_step(carry, q2, kc, vc, bias, k_is_transposed=False):
    m, l, acc = carry
    if k_is_transposed:
        s = jnp.dot(q2, kc, preferred_element_type=F32) + bias
    else:
        s = lax.dot_general(q2, kc, (((1,), (1,)), ((), ())), preferred_element_type=F32) + bias
    m_new = jnp.maximum(m, jnp.max(s, axis=-1, keepdims=True))
    alpha = jnp.exp(m - m_new)
    p = jnp.exp(s - m_new)
    l = alpha * l + jnp.sum(p, axis=-1, keepdims=True)
    acc = alpha * acc + jnp.dot(p.astype(BF16), vc, preferred_element_type=F32)
    return m_new, l, acc


def _stack_maps(q, scale):
    lane = lax.broadcasted_iota(jnp.int32, q.shape, 1)
    qs = q * scale
    q0 = jnp.where(lane < DK, qs, 0.0)
    q1 = jnp.where(lane >= DK, qs, 0.0)
    return jnp.concatenate([q0, q1], axis=0).astype(BF16)


def _diff_out(m, l, acc, n, lam, sg, lam_init):
    o = acc[:n] / l[:n] - lam * (acc[n:] / l[n:])
    return _rms(o, sg) * (1.0 - lam_init)


def _prompt_attn_body(slope_ref, lq_ref, lk_ref, sg_ref, q_ref, k_ref, v_ref, o_ref,
                      kb, vb, nb_s, *, lam_init):
    slope = slope_ref[0, 0:1, 0:1]
    lam = _lam_from(lq_ref, lk_ref, lam_init)
    sg = sg_ref[...]
    scale = DK ** -0.5
    kb[...] = k_ref[...].astype(BF16)
    vb[:, 0:DV] = v_ref[...].astype(BF16)
    neg_inf = float("-inf")

    zpad = jnp.zeros((LANES - N_META, LANES), BF16)
    k_meta = jnp.concatenate([kb[0:N_META, :], zpad], axis=0)
    v_meta = jnp.concatenate([vb[0:N_META, 0:DV], zpad], axis=0)

    rm = lax.broadcasted_iota(jnp.int32, (2 * N_META, LANES), 0) % N_META
    cm = lax.broadcasted_iota(jnp.int32, (2 * N_META, LANES), 1)
    bias_m = jnp.where(cm <= rm, -slope * (rm - cm).astype(F32), neg_inf)
    q2 = _stack_maps(q_ref[0:N_META, :], scale)
    init = (jnp.full((2 * N_META, 1), neg_inf, F32), jnp.zeros((2 * N_META, 1), F32),
            jnp.zeros((2 * N_META, LANES), F32))
    m, l, acc = _softmax_step(init, q2, k_meta, v_meta, bias_m)
    o_ref[0:N_META, :] = _diff_out(m, l, acc, N_META, lam, sg, lam_init)

    sl2 = slope * LOG2E
    vb[:, DV:2 * DV] = jnp.ones((T_P, DV), BF16)
    rr = lax.broadcasted_iota(jnp.int32, (2 * ATT_T, ATT_T), 0) % ATT_T
    cc = lax.broadcasted_iota(jnp.int32, (2 * ATT_T, ATT_T), 1)
    nbase = -sl2 * (rr - cc).astype(F32)
    nb_s[0] = nbase
    nb_s[1] = jnp.where(cc <= rr, nbase, neg_inf)
    nb_meta = jnp.where(cc < N_META, nbase, neg_inf)

    for i in range(ATT_NT):
        q0 = N_META + i * ATT_T
        q2 = _stack_maps(q_ref[q0:q0 + ATT_T, :], scale * LOG2E)
        tiles = [(0, None, float(N_META + i * ATT_T))]
        tiles += [(N_META + j * ATT_T, 0, float((i - j) * ATT_T)) for j in range(i)]
        tiles += [(q0, 1, 0.0)]
        scores, mx = [], jnp.full((2 * ATT_T, LANES), neg_inf, F32)
        for k0, kind, off in tiles:
            s = lax.dot_general(q2, kb[k0:k0 + ATT_T, :], NT_DIMS, preferred_element_type=F32)
            s = s + (nb_meta if kind is None else nb_s[kind])
            mx = jnp.maximum(mx, jnp.maximum(s[:, 0:LANES], s[:, LANES:2 * LANES]) - sl2 * off)
            scores.append(s)
        m = jnp.max(mx, axis=-1, keepdims=True)
        probs = []
        for (k0, kind, off), s in zip(tiles, scores):
            probs.append(jnp.exp2(s - (m + sl2 * off)).astype(BF16))
        p_all = jnp.concatenate(probs, axis=1)
        v_all = jnp.concatenate([vb[0:ATT_T, :], vb[N_META:q0 + ATT_T, :]], axis=0)
        acc = jnp.dot(p_all, v_all, preferred_element_type=F32)
        o0 = acc[0:ATT_T, 0:DV] / acc[0:ATT_T, DV:2 * DV]
        o1 = acc[ATT_T:2 * ATT_T, 0:DV] / acc[ATT_T:2 * ATT_T, DV:2 * DV]
        o_ref[q0:q0 + ATT_T, :] = _rms(o0 - lam * o1, sg) * (1.0 - lam_init)


def _prompt_attn(q_all, k_all, v_all, slopes, lq, lk, sg, lam_init):
    blk = pl.BlockSpec((T_P, DV), lambda b, h: (b, h))
    return pl.pallas_call(
        functools.partial(_prompt_attn_body, lam_init=lam_init),
        grid=(BATCH, N_DH),
        in_specs=[pl.BlockSpec((1, 1, LANES), lambda b, h: (h, 0, 0)),
                  pl.BlockSpec((2, DK), lambda b, h: (0, 0)),
                  pl.BlockSpec((2, DK), lambda b, h: (0, 0)),
                  pl.BlockSpec((1, DV), lambda b, h: (0, 0)),
                  blk, blk, blk],
        out_specs=blk,
        out_shape=jax.ShapeDtypeStruct((ROWS_P, D_ATT), F32),
        scratch_shapes=[pltpu.VMEM((T_P, DV), BF16),
                        pltpu.VMEM((T_P, 2 * DV), BF16),
                        pltpu.VMEM((2, 2 * ATT_T, ATT_T), F32)],
        compiler_params=_cparams(("parallel", "parallel")),
        name="prompt_attn",
    )(slopes, lq, lk, sg.reshape(1, DV), q_all, k_all, v_all)


def _decode_attn_body(pt_ref, srow_ref, qi_ref, lq_ref, lk_ref, sg_ref, q_ref, kn_ref, vn_ref, *refs,
                      lam_init, n_pages):
    k_refs, v_refs = refs[:DEC_PP], refs[DEC_PP:2 * DEC_PP]
    o_ref, m_s, l_s, acc_s = refs[2 * DEC_PP:]
    s_id = pl.program_id(1)
    n_rows = 2 * N_DH * DEC_SEQ
    past_len = n_pages * PAGE_SIZE
    neg_inf = float("-inf")

    row = lax.broadcasted_iota(jnp.int32, (n_rows, D_ATT), 0)
    lane = lax.broadcasted_iota(jnp.int32, (n_rows, D_ATT), 1)
    q2 = jnp.where(lane // DK == row // DEC_SEQ, q_ref[0] * (DK ** -0.5), 0.0).astype(BF16)
    srow = srow_ref[...]
    qi = qi_ref[...]
    col = lax.broadcasted_iota(jnp.int32, (n_rows, PAGE_SIZE), 1).astype(F32)
    srow_w = jnp.concatenate([srow] * DEC_PP, axis=1)
    qi_w = jnp.concatenate([qi] * DEC_PP, axis=1)
    col_w = lax.broadcasted_iota(jnp.int32, (n_rows, DEC_PP * PAGE_SIZE), 1).astype(F32)

    @pl.when(s_id == 0)
    def _():
        m_s[...] = jnp.full(m_s.shape, neg_inf, F32)
        l_s[...] = jnp.zeros(l_s.shape, F32)
        acc_s[...] = jnp.zeros(acc_s.shape, F32)

    def step(kc, vc, bias, k_is_transposed=False):
        m, l, acc = _softmax_step((m_s[:, 0:1], l_s[:, 0:1], acc_s[...]), q2, kc, vc, bias,
                                  k_is_transposed)
        m_s[...] = jnp.broadcast_to(m, m_s.shape)
        l_s[...] = jnp.broadcast_to(l, l_s.shape)
        acc_s[...] = acc

    kt = jnp.concatenate([k_refs[p][0].astype(BF16) for p in range(DEC_PP)], axis=1)
    vc = jnp.concatenate(
        [jnp.concatenate(
            [v_refs[p][0, pl.ds(h, PAGE_SIZE, stride=N_DH), :].astype(BF16) for h in range(N_DH)],
            axis=1) for p in range(DEC_PP)], axis=0)
    kpos0 = jnp.asarray(s_id * (DEC_PP * PAGE_SIZE), F32)
    dist = (past_len + qi_w) - (kpos0 + col_w)
    step(kt, vc, -srow_w * dist, k_is_transposed=True)

    @pl.when(s_id == pl.num_programs(1) - 1)
    def _():
        dist = qi - col
        bias = jnp.where(dist >= 0, -srow * dist, neg_inf)
        step(kn_ref[0].astype(BF16), vn_ref[0].astype(BF16), bias)
        lam = _lam_from(lq_ref, lk_ref, lam_init)
        sg = sg_ref[...]
        l = l_s[:, 0:1]
        for h in range(N_DH):
            blk = acc_s[h * 2 * DEC_SEQ:(h + 1) * 2 * DEC_SEQ, h * DV:(h + 1) * DV]
            blk = blk / l[h * 2 * DEC_SEQ:(h + 1) * 2 * DEC_SEQ]
            o = blk[0:DEC_SEQ] - lam * blk[DEC_SEQ:2 * DEC_SEQ]
            o_ref[0, :, h * DV:(h + 1) * DV] = _rms(o, sg) * (1.0 - lam_init)


def _decode_attn(page_table, q_s, k_s, v_s, cache_k, cache_v, slopes_vec, lq, lk, sg, lam_init):
    n_pages = page_table.shape[1]
    n_rows = 2 * N_DH * DEC_SEQ
    q_t = jnp.tile(q_s, (1, 2 * N_DH, 1))
    pad = ((0, 0), (0, PAGE_SIZE - DEC_SEQ), (0, 0))
    kn = jnp.pad(k_s, pad)
    vn = jnp.pad(v_s, pad)
    ridx = jnp.arange(n_rows)
    srow = jnp.broadcast_to(slopes_vec[ridx // (2 * DEC_SEQ)][:, None], (n_rows, PAGE_SIZE)).astype(F32)
    qi = jnp.broadcast_to((ridx % DEC_SEQ)[:, None], (n_rows, PAGE_SIZE)).astype(F32)

    def page_spec(p, shape):
        return pl.BlockSpec((1,) + shape, lambda b, s, pt: (pt[b, s * DEC_PP + p], 0, 0))

    const2 = lambda b, s, pt: (0, 0)
    per_b = lambda b, s, pt: (b, 0, 0)
    grid_spec = pltpu.PrefetchScalarGridSpec(
        num_scalar_prefetch=1,
        grid=(DEC_BATCH, n_pages // DEC_PP),
        in_specs=[pl.BlockSpec((n_rows, PAGE_SIZE), const2),
                  pl.BlockSpec((n_rows, PAGE_SIZE), const2),
                  pl.BlockSpec((2, DK), const2),
                  pl.BlockSpec((2, DK), const2),
                  pl.BlockSpec((1, DV), const2),
                  pl.BlockSpec((1, n_rows, D_ATT), per_b),
                  pl.BlockSpec((1, PAGE_SIZE, D_ATT), per_b),
                  pl.BlockSpec((1, PAGE_SIZE, D_ATT), per_b)]
                 + [page_spec(p, (D_ATT, PAGE_SIZE)) for p in range(DEC_PP)]
                 + [page_spec(p, (PAGE_SIZE * N_DH, DV)) for p in range(DEC_PP)],
        out_specs=pl.BlockSpec((1, DEC_SEQ, D_ATT), per_b),
        scratch_shapes=[pltpu.VMEM((n_rows, LANES), F32), pltpu.VMEM((n_rows, LANES), F32),
                        pltpu.VMEM((n_rows, D_ATT), F32)],
    )
    return pl.pallas_call(
        functools.partial(_decode_attn_body, lam_init=lam_init, n_pages=n_pages),
        grid_spec=grid_spec,
        out_shape=jax.ShapeDtypeStruct((DEC_BATCH, DEC_SEQ, D_ATT), F32),
        compiler_params=_cparams(("parallel", "arbitrary")),
        name="decode_attn",
    )(page_table, srow, qi, lq, lk, sg.reshape(1, DV), q_t, kn, vn,
      *([cache_k] * DEC_PP), *([cache_v] * DEC_PP))


CONV_PRE = 32
CONV_RC = 48
CONV_CT = 256
assert T_P % CONV_RC == 0


def _conv_prompt_body(prev_ref, g_ref, w_ref, b_ref, o_ref, gp, wr_s):
    gp[0:CONV_PRE, :] = prev_ref[0]
    gp[CONV_PRE:, :] = g_ref[...]
    lead = CONV_PRE - (CONV_W - 1)
    win_rows = CONV_RC + CONV_PRE
    bias = b_ref[...]

    def chunk(ci, _):
        t0 = pl.multiple_of(ci * CONV_RC, SUBLANES)
        win = gp[pl.ds(t0, win_rows), :]
        acc = jnp.broadcast_to(bias, (CONV_RC, CONV_CT))
        for r in range(SUBLANES):
            taps = [k for k in range(CONV_W) if (k + lead) % SUBLANES == r]
            if not taps:
                continue
            hi = max(k + lead for k in taps) - r + CONV_RC
            if r:
                wr_s[0:hi, :] = win[r:r + hi, :]
            for k in taps:
                a = k + lead - r
                src = wr_s[a:a + CONV_RC, :] if r else win[a:a + CONV_RC, :]
                acc = acc + w_ref[k:k + 1, :] * src
        o_ref[pl.ds(t0, CONV_RC), :] = acc
        return 0

    lax.fori_loop(0, T_P // CONV_RC, chunk, 0)


def _conv_prompt(g_all, prev, w, b):
    return pl.pallas_call(
        _conv_prompt_body,
        grid=(BATCH, D_CONV // CONV_CT),
        in_specs=[pl.BlockSpec((1, CONV_PRE, CONV_CT), lambda bi, j: (bi, 0, j)),
                  pl.BlockSpec((T_P, CONV_CT), lambda bi, j: (bi, j)),
                  pl.BlockSpec((CONV_W, CONV_CT), lambda bi, j: (0, j)),
                  pl.BlockSpec((1, CONV_CT), lambda bi, j: (0, j))],
        out_specs=pl.BlockSpec((T_P, CONV_CT), lambda bi, j: (bi, j)),
        out_shape=jax.ShapeDtypeStruct((ROWS_P, D_CONV), F32),
        scratch_shapes=[pltpu.VMEM((CONV_PRE + T_P, CONV_CT), F32),
                        pltpu.VMEM((CONV_RC + CONV_PRE, CONV_CT), F32)],
        compiler_params=_cparams(("parallel", "parallel")),
        name="conv_prompt",
    )(prev, g_all, w, b.reshape(1, D_CONV))


def _conv_sample_body(ext_ref, w_ref, b_ref, o_ref):
    w = w_ref[...]
    for bi in range(DEC_BATCH):
        for t in range(DEC_SEQ):
            acc = jnp.sum(w * ext_ref[bi, t:t + CONV_W, :], axis=0, keepdims=True) + b_ref[...]
            o_ref[bi * DEC_SEQ + t:bi * DEC_SEQ + t + 1, :] = acc


def _conv_sample(ext, w, b):
    return pl.pallas_call(
        _conv_sample_body,
        out_shape=jax.ShapeDtypeStruct((ROWS_S, D_CONV), F32),
        name="conv_sample",
    )(ext, w, b.reshape(1, D_CONV))


SSM_NK = SSM_L + 1


def _ssm_param_body(lre_ref, lim_ref, ldt_ref, bre_ref, bim_ref, cre_ref, cim_ref,
                    ap_re, ap_im, bp_rr, bp_ii, cp_rr, cp_ii, km_ref,
                    ar_s, ai_s, pr_s, pi_s, bbr_s, bbi_s):
    k = pl.program_id(0)

    @pl.when(k == 0)
    def _():
        lr, li = lre_ref[...], lim_ref[...]
        dt = jnp.exp(ldt_ref[...])
        er = jnp.exp(lr * dt)
        ar = er * jnp.cos(li * dt)
        ai = er * jnp.sin(li * dt)
        den = lr * lr + li * li
        xr, xi = ar - 1.0, ai
        cr = (xr * lr + xi * li) / den
        ci = (xi * lr - xr * li) / den
        br, bi = bre_ref[...], bim_ref[...]
        bbr_s[...] = cr * br - ci * bi
        bbi_s[...] = cr * bi + ci * br
        ar_s[...] = ar
        ai_s[...] = ai
        pr_s[...] = jnp.ones(pr_s.shape, F32)
        pi_s[...] = jnp.zeros(pi_s.shape, F32)

    @pl.when(k > 0)
    def _():
        pr, pi = pr_s[...], pi_s[...]
        ar, ai = ar_s[...], ai_s[...]
        pr_s[...] = pr * ar - pi * ai
        pi_s[...] = pr * ai + pi * ar

    pr, pi = pr_s[...], pi_s[...]
    ap_re[0] = pr
    ap_im[0] = pi
    bbr, bbi = bbr_s[...], bbi_s[...]
    bpr = pr * bbr - pi * bbi
    bpi = pr * bbi + pi * bbr
    bp_rr[0] = jnp.concatenate([bpr, bpr], axis=-1)
    bp_ii[0] = jnp.concatenate([bpi, bpi], axis=-1)
    cr, ci = cre_ref[...], cim_ref[...]
    cpr = pr * cr - pi * ci
    cpi = -(pr * ci + pi * cr)
    cp_rr[0] = jnp.concatenate([cpr, cpr], axis=-1)
    cp_ii[0] = jnp.concatenate([cpi, cpi], axis=-1)
    dn = (((2,), (2,)), ((0,), (0,)))
    hp = lax.Precision.HIGHEST
    km_ref[0] = (lax.dot_general(bpr, cr, dn, precision=hp, preferred_element_type=F32)
                 - lax.dot_general(bpi, ci, dn, precision=hp, preferred_element_type=F32))


def _ssm_params(lre, lim, log_dt, b_re, b_im, c_re, c_im):
    g, p, c = N_SSM_GROUPS, SSM_P, SSM_GROUP
    bc = lambda a: jnp.broadcast_to(a[:, None, :], (g, c, p))
    ldt = jnp.broadcast_to(log_dt[:, None, None], (g, c, p))
    bt_re = jnp.swapaxes(b_re, 1, 2)
    bt_im = jnp.swapaxes(b_im, 1, 2)
    gcp = pl.BlockSpec((g, c, p), lambda k: (0, 0, 0))
    o_gcp = pl.BlockSpec((1, g, c, p), lambda k: (k, 0, 0, 0))
    o_wide = pl.BlockSpec((1, g, c, 2 * p), lambda k: (k, 0, 0, 0))
    narrow = jax.ShapeDtypeStruct((SSM_NK, g, c, p), F32)
    wide = jax.ShapeDtypeStruct((SSM_NK, g, c, 2 * p), F32)
    ap_re, ap_im, bp_rr, bp_ii, cp_rr, cp_ii, km = pl.pallas_call(
        _ssm_param_body,
        grid=(SSM_NK,),
        in_specs=[gcp] * 7,
        out_specs=[o_gcp] * 2 + [o_wide] * 4 + [pl.BlockSpec((1, g, c, c), lambda k: (k, 0, 0, 0))],
        out_shape=[narrow] * 2 + [wide] * 4 + [jax.ShapeDtypeStruct((SSM_NK, g, c, c), F32)],
        scratch_shapes=[pltpu.VMEM((g, c, p), F32)] * 6,
        compiler_params=_cparams(("arbitrary",)),
        name="ssm_params",
    )(bc(lre), bc(lim), ldt, bt_re, bt_im, c_re, c_im)
    flat = lambda a: a.reshape(SSM_NK, g * c, 2 * p)
    return (ap_re[:, :, 0, :].reshape(SSM_NK, g * p), ap_im[:, :, 0, :].reshape(SSM_NK, g * p),
            flat(bp_rr), flat(bp_ii), flat(cp_rr), flat(cp_ii), km.reshape(SSM_NK, g * c, c))


def _gather_chunks(u_ref, length, m):
    xs = [u_ref[pl.ds(s, m, stride=length), :].astype(BF16) for s in range(length)]
    return jnp.concatenate(xs, axis=1)


SSM_SQ = 2 * SSM_SW // LANES
SSM_HQ = SSM_SQ // 2


def _ssm_body(u_ref, km_ref, bpr_ref, bpi_ref, cpr_ref, cpi_ref, are_ref, aim_ref, h0r_ref, h0i_ref,
              d_ref, *rest, length, m, n_seq, n_chunks, in_place):
    if in_place:
        rest = rest[1:]
    o_ref, finr_ref, fini_ref, v_s, ws_s, wct_s, s_s, hp_s = rest
    x = _gather_chunks(u_ref, length, m)

    erow = lax.broadcasted_iota(jnp.int32, (SSM_GROUP, LANES), 0)
    elane = lax.broadcasted_iota(jnp.int32, (SSM_GROUP, LANES), 1)
    spread = jnp.where(elane % SSM_GROUP == erow, 1.0, 0.0).astype(BF16)
    rgrp = lax.broadcasted_iota(jnp.int32, (LANES, LANES), 0) // SSM_GROUP
    lane = lax.broadcasted_iota(jnp.int32, (LANES, LANES), 1)
    for q in range(length):
        blk = jnp.dot(km_ref[length - 1 - q].astype(BF16), spread, preferred_element_type=F32)
        v_s[q * LANES:(q + 1) * LANES, :] = jnp.where(rgrp == lane // SSM_GROUP, blk, 0.0).astype(BF16)
    for s in range(length):
        rows = slice(s * LANES, (s + 1) * LANES)
        for q in range(SSM_HQ):
            own = rgrp == 2 * q + lane // SSM_P
            re_t = slice(q * LANES, (q + 1) * LANES)
            im_t = slice((SSM_HQ + q) * LANES, (SSM_HQ + q + 1) * LANES)
            ws_s[rows, re_t] = jnp.where(own, bpr_ref[length - 1 - s], 0.0).astype(BF16)
            ws_s[rows, im_t] = jnp.where(own, bpi_ref[length - 1 - s], 0.0).astype(BF16)
            wct_s[rows, re_t] = jnp.where(own, cpr_ref[s + 1], 0.0).astype(BF16)
            wct_s[rows, im_t] = jnp.where(own, cpi_ref[s + 1], 0.0).astype(BF16)

    s_loc = jnp.dot(x, ws_s[...], preferred_element_type=F32)
    for q in range(SSM_SQ):
        s_s[q] = s_loc[:, q * LANES:(q + 1) * LANES]

    if m != n_seq * n_chunks:
        hp_s[...] = jnp.zeros(hp_s.shape, F32)
    ar = [are_ref[:, q * LANES:(q + 1) * LANES] for q in range(SSM_HQ)]
    ai = [aim_ref[:, q * LANES:(q + 1) * LANES] for q in range(SSM_HQ)]

    def advance(rows, hs):
        out_r, out_i = [], []
        for q in range(SSM_HQ):
            hr, hi = hs[q], hs[SSM_HQ + q]
            hp_s[q, rows, :] = hr
            hp_s[SSM_HQ + q, rows, :] = hi
            out_r.append(ar[q] * hr - ai[q] * hi + s_s[q, rows, :])
            out_i.append(ar[q] * hi + ai[q] * hr + s_s[SSM_HQ + q, rows, :])
        return tuple(out_r + out_i)

    hs = tuple([h0r_ref[:, q * LANES:(q + 1) * LANES] for q in range(SSM_HQ)]
               + [h0i_ref[:, q * LANES:(q + 1) * LANES] for q in range(SSM_HQ)])
    if n_chunks == 1:
        hs = advance(pl.ds(0, n_seq), hs)
    else:
        hs = lax.fori_loop(0, n_chunks,
                           lambda c, hs: advance(pl.ds(c, n_seq, stride=n_chunks), hs), hs)
    for q in range(SSM_HQ):
        finr_ref[:, q * LANES:(q + 1) * LANES] = hs[q]
        fini_ref[:, q * LANES:(q + 1) * LANES] = hs[SSM_HQ + q]

    hp = jnp.concatenate([hp_s[q].astype(BF16) for q in range(SSM_SQ)], axis=1)
    d = d_ref[...]
    for t in range(length):
        rows = pl.ds(t, m, stride=length)
        y = jnp.dot(x[:, :(t + 1) * LANES], v_s[(length - 1 - t) * LANES:, :],
                    preferred_element_type=F32)
        y = y + lax.dot_general(hp, wct_s[t * LANES:(t + 1) * LANES, :], NT_DIMS,
                                preferred_element_type=F32)
        o_ref[rows, :] = y + d * u_ref[rows, :]


def _ssm_apply(u_all, y_all, ops, a_re, a_im, h0_re, h0_im, dvec, *, length, m, n_seq, n_chunks,
               row_blk):
    in_place = y_all is not None
    bp_rr, bp_ii, cp_rr, cp_ii, km = ops
    rows = pl.BlockSpec((m * length, LANES), lambda j: (row_blk, j))
    comp = lambda last: pl.BlockSpec((SSM_NK, LANES, last), lambda j: (0, j, 0))
    vec = pl.BlockSpec((1, SSM_SW), lambda j: (0, j))
    st = pl.BlockSpec((n_seq, SSM_SW), lambda j: (0, j))
    in_specs = [rows, comp(SSM_GROUP)] + [comp(LANES)] * 4 + [vec, vec, st, st,
                pl.BlockSpec((1, LANES), lambda j: (0, j))]
    args = [u_all, km, bp_rr, bp_ii, cp_rr, cp_ii, a_re, a_im, h0_re, h0_im, dvec]
    if in_place:
        in_specs.append(pl.BlockSpec(memory_space=pl.ANY))
        args.append(y_all)
    return pl.pallas_call(
        functools.partial(_ssm_body, length=length, m=m, n_seq=n_seq, n_chunks=n_chunks,
                          in_place=in_place),
        grid=(SSM_NJ,),
        in_specs=in_specs,
        out_specs=[rows, st, st],
        out_shape=[jax.ShapeDtypeStruct((ROWS, D_MODEL), F32),
                   jax.ShapeDtypeStruct((n_seq, N_SSM_GROUPS * SSM_P), F32),
                   jax.ShapeDtypeStruct((n_seq, N_SSM_GROUPS * SSM_P), F32)],
        input_output_aliases={len(args) - 1: 0} if in_place else {},
        scratch_shapes=[pltpu.VMEM((length * LANES, LANES), BF16),
                        pltpu.VMEM((length * LANES, SSM_SQ * LANES), BF16),
                        pltpu.VMEM((length * LANES, SSM_SQ * LANES), BF16),
                        pltpu.VMEM((SSM_SQ, m, LANES), F32),
                        pltpu.VMEM((SSM_SQ, m, LANES), F32)],
        compiler_params=_cparams(("parallel",)),
        name="ssm_sample" if in_place else "ssm_prompt",
    )(*args)


def kernel(x_prompt, x_sample, cache_k, cache_v, state_conv, state_ssm_re, state_ssm_im, page_table, meta_tokens, norm_mix_pre, norm_mix_post, norm_ffn_pre, norm_ffn_post, w_in_even, lambda_q, lambda_k, subln_g, conv_w, conv_b, conv_ln_g, conv_ln_b, w_out_even, w_in_odd, ssm_a_re, ssm_a_im, ssm_b_re, ssm_b_im, ssm_c_re, ssm_c_im, ssm_d, ssm_log_dt, w_glu, w_out_odd, w_ffn_gate, w_ffn_up, w_ffn_down):
    n_pool = cache_k.shape[1]
    meta = jnp.broadcast_to(meta_tokens.astype(F32)[None], (BATCH, N_META, D_MODEL))
    x = jnp.concatenate([
        jnp.concatenate([meta, x_prompt], axis=1).reshape(ROWS_P, D_MODEL),
        x_sample.reshape(ROWS_S, D_MODEL),
        jnp.zeros((ROWS - ROWS_P - ROWS_S, D_MODEL), F32)], axis=0)
    slopes_vec = jnp.array([2.0 ** (-8.0 * (i + 1) / N_DH) for i in range(N_DH)], dtype=F32)
    slopes = jnp.broadcast_to(slopes_vec[:, None, None], (N_DH, 1, LANES))

    ks, vs, convs_p, convs_s, sre_p, sim_p, sre_s, sim_s = [], [], [], [], [], [], [], []
    for layer in range(DEPTH):
        if layer % 2 == 0:
            e = layer // 2
            lam_init = 0.8 - 0.6 * math.exp(-0.3 * (2 * e))
            w_in = w_in_even[e].astype(BF16)
            hb = _rms_cast(x, norm_mix_pre[layer])
            q_p, q_t = _mm(hb, [w_in[:, 0:D_ATT]], split=True)
            k_p, k_t = _mm(hb, [w_in[:, D_ATT:2 * D_ATT]], split=True)
            v_p, v_t = _mm(hb, [w_in[:, 2 * D_ATT:3 * D_ATT]], split=True)
            g_p, g_t = _mm(hb, [w_in[:, 3 * D_ATT:3 * D_ATT + D_CONV], w_in[:, 3 * D_ATT + D_CONV:]],
                           split=True)
            per_seq = lambda a: a[:ROWS_S].reshape(DEC_BATCH, DEC_SEQ, a.shape[-1])
            pad_rows = jnp.zeros((ROWS_T - ROWS_S, D_ATT), F32)
            o_p = _prompt_attn(q_p, k_p, v_p, slopes, lambda_q[e], lambda_k[e], subln_g[e], lam_init)
            o_s = _decode_attn(page_table, per_seq(q_t), per_seq(k_t), per_seq(v_t),
                               jnp.transpose(cache_k[e], (0, 2, 3, 4, 1)).reshape(n_pool, D_ATT, PAGE_SIZE),
                               cache_v[e].reshape(n_pool, PAGE_SIZE * N_DH, DV),
                               slopes_vec, lambda_q[e], lambda_k[e], subln_g[e], lam_init)
            o_tail = jnp.concatenate([o_s.reshape(ROWS_S, D_ATT), pad_rows], axis=0)
            ext_s = jnp.concatenate([state_conv[e], per_seq(g_t)], axis=1)
            c_p = _conv_prompt(g_p, jnp.zeros((BATCH, CONV_PRE, D_CONV), F32), conv_w[e], conv_b[e])
            c_s = _conv_sample(ext_s, conv_w[e], conv_b[e])
            c_tail = jnp.concatenate([c_s, pad_rows], axis=0)
            w_out = w_out_even[e].astype(BF16)
            x = _even_out(o_p, c_p, o_tail, c_tail, conv_ln_g[e], conv_ln_b[e], w_out[:D_ATT],
                          w_out[D_ATT:], norm_mix_post[layer], x)
            k_tok_minor = _mm_t(hb, jnp.transpose(w_in[:, D_ATT:2 * D_ATT]))
            k_prompt = jnp.transpose(k_tok_minor.reshape(BATCH, N_DH, 2, DK, T_P), (0, 4, 1, 2, 3))
            ks.append((k_prompt, per_seq(k_t).reshape(DEC_BATCH, DEC_SEQ, N_DH, 2, DK)))
            vs.append((v_p.reshape(BATCH, T_P, N_DH, DV), per_seq(v_t).reshape(DEC_BATCH, DEC_SEQ, N_DH, DV)))
            convs_p.append(g_p.reshape(BATCH, T_P, D_CONV)[:, T_P - (CONV_W - 1):])
            convs_s.append(ext_s[:, DEC_SEQ:])
        else:
            o = layer // 2
            u_all = _mm(_rms_cast(x, norm_mix_pre[layer]), [w_in_odd[o].astype(BF16)])
            gp = N_SSM_GROUPS * SSM_P
            ap_re, ap_im, *ops = _ssm_params(
                ssm_a_re[o], ssm_a_im[o], ssm_log_dt[o], ssm_b_re[o], ssm_b_im[o],
                ssm_c_re[o], ssm_c_im[o])
            dvec = ssm_d[o].reshape(1, D_MODEL)
            zeros = jnp.zeros((BATCH, gp), F32)
            y_all, pr, pi = _ssm_apply(
                u_all, None, ops, ap_re[SSM_L:SSM_L + 1], ap_im[SSM_L:SSM_L + 1], zeros, zeros, dvec,
                length=SSM_L, m=SSM_M, n_seq=BATCH, n_chunks=SSM_NC, row_blk=0)
            y_all, sr, si = _ssm_apply(
                u_all, y_all, ops, ap_re[DEC_SEQ:DEC_SEQ + 1], ap_im[DEC_SEQ:DEC_SEQ + 1],
                state_ssm_re[o].reshape(DEC_BATCH, gp), state_ssm_im[o].reshape(DEC_BATCH, gp), dvec,
                length=DEC_SEQ, m=DEC_BATCH, n_seq=DEC_BATCH, n_chunks=1, row_blk=SAMPLE_BLK)
            x = _odd_out(y_all, x, w_glu[o].astype(BF16), w_out_odd[o].astype(BF16),
                         norm_mix_post[layer])
            sre_p.append(pr.reshape(BATCH, N_SSM_GROUPS, SSM_P))
            sim_p.append(pi.reshape(BATCH, N_SSM_GROUPS, SSM_P))
            sre_s.append(sr.reshape(DEC_BATCH, N_SSM_GROUPS, SSM_P))
            sim_s.append(si.reshape(DEC_BATCH, N_SSM_GROUPS, SSM_P))
        x = _ffn(x, norm_ffn_pre[layer], w_ffn_gate[layer].astype(BF16), w_ffn_up[layer].astype(BF16),
                 w_ffn_down[layer].astype(BF16), norm_ffn_post[layer], split=layer == DEPTH - 1)

    x_p, x_t = x
    y_prompt = x_p.reshape(BATCH, T_P, D_MODEL)[:, N_META:]
    y_sample = x_t[:ROWS_S].reshape(DEC_BATCH, DEC_SEQ, D_MODEL)
    return (y_prompt, y_sample,
            jnp.stack([k[0] for k in ks]), jnp.stack([v[0] for v in vs]),
            jnp.stack([k[1] for k in ks]), jnp.stack([v[1] for v in vs]),
            jnp.stack(convs_p), jnp.stack(convs_s),
            jnp.stack(sre_p), jnp.stack(sim_p), jnp.stack(sre_s), jnp.stack(sim_s))
```
